```python
import math
import jax, jax.numpy as jnp
from jax import lax
import numpy as np

D_MODEL = 1024
BATCH = 16
SEQ = 2048
DEPTH = 1

CHUNK = 64
N_META = 16
Q_BLOCK = 128
EPS = 1e-6

MLA_HEADS = 8
MLA_Q_RANK = 384
MLA_KV_RANK = 256
MLA_NOPE = 128
MLA_ROPE = 64
MLA_V = 128
ROPE_HALF = MLA_ROPE // 2
ROPE_THETA = 10000.0

GLA_HEADS = 4
GLA_DK = D_MODEL // 2
GLA_DV = D_MODEL
GLA_HK = GLA_DK // GLA_HEADS
GLA_HV = GLA_DV // GLA_HEADS
GLA_GATE_RANK = 16
GLA_TAU = 16.0
GLA_CHUNK = 16

PEER_HEADS = 8
PEER_NKEYS = 128
PEER_EXPERTS = PEER_NKEYS * PEER_NKEYS
PEER_QDIM = 256
PEER_HALF = PEER_QDIM // 2
PEER_TOPK = 16
PEER_BLOCK = 16

SPLITS = (MLA_Q_RANK, MLA_KV_RANK, MLA_ROPE, GLA_DK, GLA_DK, GLA_DV, GLA_GATE_RANK, GLA_DV, D_MODEL, D_MODEL)
D_IN = MLA_Q_RANK + MLA_KV_RANK + MLA_ROPE + 2 * GLA_DK + GLA_DV + GLA_GATE_RANK + GLA_DV + 2 * D_MODEL

kernel_name = "hybrid_mla_gla_peer_meta_chunk_causal"


def rmsnorm(x, g):
    x32 = x.astype(jnp.float32)
    y = x32 * lax.rsqrt(jnp.mean(x32 * x32, axis=-1, keepdims=True) + EPS)
    return (y * g.astype(jnp.float32)).astype(x.dtype)


def rope_tables(n):
    inv_freq = ROPE_THETA ** (-jnp.arange(ROPE_HALF, dtype=jnp.float32) / ROPE_HALF)
    ang = jnp.arange(n, dtype=jnp.float32)[:, None] * inv_freq[None, :]
    return jnp.cos(ang), jnp.sin(ang)


def rope(x, cos, sin):
    x32 = x.astype(jnp.float32)
    x1, x2 = x32[..., :ROPE_HALF], x32[..., ROPE_HALF:]
    return jnp.concatenate([x1 * cos - x2 * sin, x2 * cos + x1 * sin], axis=-1).astype(x.dtype)


def mla_branch(c_q, c_kv, k_pe, q_norm, w_uq, kv_norm, w_ukv, qn_nope, qn_pe, kn_nope, kn_pe, cos, sin, cid):
    B, L, _ = c_q.shape
    q = (rmsnorm(c_q, q_norm) @ w_uq).reshape(B, L, MLA_HEADS, MLA_NOPE + MLA_ROPE)
    kv = (rmsnorm(c_kv, kv_norm) @ w_ukv).reshape(B, L, MLA_HEADS, MLA_NOPE + MLA_V)
    q_nope = rmsnorm(q[..., :MLA_NOPE], qn_nope)
    q_pe = rope(rmsnorm(q[..., MLA_NOPE:], qn_pe), cos[:, None, :], sin[:, None, :])
    k_nope = rmsnorm(kv[..., :MLA_NOPE], kn_nope)
    v = kv[..., MLA_NOPE:]
    k_pe = rope(rmsnorm(k_pe, kn_pe), cos, sin)
    scale = (MLA_NOPE + MLA_ROPE) ** -0.5
    lp = cid.shape[0]
    nb = lp // Q_BLOCK
    pad = ((0, 0), (0, lp - L), (0, 0), (0, 0))
    qn_b = jnp.pad(q_nope, pad).reshape(B, nb, Q_BLOCK, MLA_HEADS, MLA_NOPE).transpose(1, 0, 2, 3, 4)
    qp_b = jnp.pad(q_pe, pad).reshape(B, nb, Q_BLOCK, MLA_HEADS, MLA_ROPE).transpose(1, 0, 2, 3, 4)
    cq_b = cid.reshape(nb, Q_BLOCK)
    cid_k = cid[:L]

    def attend(blk):
        qn, qp, cq = blk
        s = (jnp.einsum('bqhd,bkhd->bhqk', qn, k_nope) + jnp.einsum('bqhd,bkd->bhqk', qp, k_pe)).astype(jnp.float32) * scale
        mask = cid_k[None, :] <= cq[:, None]
        s = jnp.where(mask, s, -jnp.inf)
        w = jax.nn.softmax(s, axis=-1).astype(v.dtype)
        return jnp.einsum('bhqk,bkhd->bqhd', w, v)

    out = lax.map(attend, (qn_b, qp_b, cq_b))
    out = out.transpose(1, 0, 2, 3, 4).reshape(B, lp, MLA_HEADS * MLA_V)
    return out[:, :L]


def gla_branch(gq, gk, gv, ga, gg, w_a2, b_a, g_norm):
    B, L, _ = gq.shape
    nc = L // GLA_CHUNK
    q = gq.reshape(B, L, GLA_HEADS, GLA_HK) * (GLA_HK ** -0.5)
    k = gk.reshape(B, L, GLA_HEADS, GLA_HK)
    v = gv.reshape(B, L, GLA_HEADS, GLA_HV)
    log_a = jax.nn.log_sigmoid((ga @ w_a2 + b_a).astype(jnp.float32)) / GLA_TAU
    log_a = log_a.reshape(B, L, GLA_HEADS, GLA_HK)

    def to_chunks(t):
        return t.reshape(B, nc, GLA_CHUNK, GLA_HEADS, t.shape[-1]).transpose(1, 0, 3, 2, 4).astype(jnp.float32)

    tri = jnp.tril(jnp.ones((GLA_CHUNK, GLA_CHUNK), dtype=bool))

    def step(S, inp):
        qc, kc, vc, lac = inp
        bcum = jnp.cumsum(lac, axis=2)
        o_inter = jnp.einsum('bhcd,bhde->bhce', qc * jnp.exp(bcum), S)
        diff = bcum[:, :, :, None, :] - bcum[:, :, None, :, :]
        decay = jnp.exp(jnp.where(tri[None, None, :, :, None], diff, -jnp.inf))
        A = jnp.einsum('bhid,bhjd,bhijd->bhij', qc, kc, decay)
        o_intra = jnp.einsum('bhij,bhje->bhie', A, vc)
        b_last = bcum[:, :, -1]
        k_dec = kc * jnp.exp(b_last[:, :, None, :] - bcum)
        S = jnp.exp(b_last)[..., None] * S + jnp.einsum('bhcd,bhce->bhde', k_dec, vc)
        return S, o_inter + o_intra

    S0 = jnp.zeros((B, GLA_HEADS, GLA_HK, GLA_HV), jnp.float32)
    _, o = lax.scan(step, S0, (to_chunks(q), to_chunks(k), to_chunks(v), to_chunks(log_a)))
    o = o.transpose(1, 0, 3, 2, 4).reshape(B, L, GLA_HEADS, GLA_HV).astype(gq.dtype)
    o = rmsnorm(o, g_norm).reshape(B, L, GLA_DV)
    return o * jax.nn.silu(gg)


def peer_ffn(h, w_q, sub_keys, u_tab, v_tab):
    B, L, D = h.shape
    q = (h @ w_q).reshape(B, L, PEER_HEADS, 2, PEER_HALF)
    s = jnp.einsum('blhpd,pnd->blhpn', q, sub_keys).astype(jnp.float32)
    top_s, top_i = lax.top_k(s, PEER_TOPK)
    cand_s = top_s[..., 0, :, None] + top_s[..., 1, None, :]
    cand_i = top_i[..., 0, :, None] * PEER_NKEYS + top_i[..., 1, None, :]
    best_s, best_pos = lax.top_k(cand_s.reshape(B, L, PEER_HEADS, PEER_TOPK * PEER_TOPK), PEER_TOPK)
    idx = jnp.take_along_axis(cand_i.reshape(B, L, PEER_HEADS, PEER_TOPK * PEER_TOPK), best_pos, axis=-1)
    gate = jax.nn.softmax(best_s, axis=-1)
    nb = L // PEER_BLOCK

    def blocks(t):
        return t.reshape((B, nb, PEER_BLOCK) + t.shape[2:]).swapaxes(0, 1)

    def expert_block(inp):
        hb, ib, gb = inp
        u = u_tab[ib]
        act = jax.nn.gelu(jnp.einsum('bpd,bphkd->bphk', hb, u).astype(jnp.float32), approximate=False)
        coef = (act * gb).astype(hb.dtype)
        return jnp.einsum('bphk,bphkd->bpd', coef, v_tab[ib])

    out = lax.map(expert_block, (blocks(h), blocks(idx), blocks(gate)))
    return out.swapaxes(0, 1).reshape(B, L, D)


def setup_inputs(seed: int = 0) -> dict:
    key = jax.random.key(seed)
    ks = jax.random.split(key, 24)
    f32 = jnp.float32

    def nrm(k, shape, fan_in):
        return jax.random.normal(k, shape, f32) * (fan_in ** -0.5)

    def gain(k, shape):
        return 1.0 + 0.02 * jax.random.normal(k, shape, f32)

    return {
        "x": jax.random.normal(ks[0], (BATCH, SEQ, D_MODEL), f32),
        "meta": jax.random.normal(ks[1], (N_META, D_MODEL), f32),
        "norm_mix": gain(ks[2], (DEPTH, D_MODEL)),
        "w_in": nrm(ks[3], (DEPTH, D_MODEL, D_IN), D_MODEL),
        "mla_q_norm": gain(ks[4], (DEPTH, MLA_Q_RANK)),
        "mla_w_uq": nrm(ks[5], (DEPTH, MLA_Q_RANK, MLA_HEADS * (MLA_NOPE + MLA_ROPE)), MLA_Q_RANK),
        "mla_kv_norm": gain(ks[6], (DEPTH, MLA_KV_RANK)),
        "mla_w_ukv": nrm(ks[7], (DEPTH, MLA_KV_RANK, MLA_HEADS * (MLA_NOPE + MLA_V)), MLA_KV_RANK),
        "qn_nope": gain(ks[8], (DEPTH, MLA_NOPE)),
        "qn_pe": gain(ks[9], (DEPTH, MLA_ROPE)),
        "kn_nope": gain(ks[10], (DEPTH, MLA_NOPE)),
        "kn_pe": gain(ks[11], (DEPTH, MLA_ROPE)),
        "gla_w_a2": nrm(ks[12], (DEPTH, GLA_GATE_RANK, GLA_DK), GLA_GATE_RANK),
        "gla_b_a": 0.1 * jax.random.normal(ks[13], (DEPTH, GLA_DK), f32),
        "gla_norm": gain(ks[14], (DEPTH, GLA_HV)),
        "w_o_mla": nrm(ks[15], (DEPTH, MLA_HEADS * MLA_V, D_MODEL), MLA_HEADS * MLA_V),
        "w_o_gla": nrm(ks[16], (DEPTH, GLA_DV, D_MODEL), GLA_DV),
        "w_out": nrm(ks[17], (DEPTH, D_MODEL, D_MODEL), D_MODEL),
        "norm_ffn": gain(ks[18], (DEPTH, D_MODEL)),
        "peer_w_q": nrm(ks[19], (DEPTH, D_MODEL, PEER_HEADS * PEER_QDIM), D_MODEL),
        "peer_keys": nrm(ks[20], (DEPTH, 2, PEER_NKEYS, PEER_HALF), PEER_HALF),
        "peer_u": nrm(ks[21], (DEPTH, PEER_EXPERTS, D_MODEL), D_MODEL),
        "peer_v": nrm(ks[22], (DEPTH, PEER_EXPERTS, D_MODEL), PEER_HEADS),
    }


def reference(x, meta, norm_mix, w_in, mla_q_norm, mla_w_uq, mla_kv_norm, mla_w_ukv, qn_nope, qn_pe, kn_nope, kn_pe, gla_w_a2, gla_b_a, gla_norm, w_o_mla, w_o_gla, w_out, norm_ffn, peer_w_q, peer_keys, peer_u, peer_v):
    B = x.shape[0]
    h = jnp.concatenate([jnp.broadcast_to(meta.astype(x.dtype)[None], (B, N_META, D_MODEL)), x], axis=1)
    L = h.shape[1]
    lp = -(-L // Q_BLOCK) * Q_BLOCK
    pos = jnp.arange(lp)
    cid = jnp.where(pos < N_META, 0, 1 + (pos - N_META) // CHUNK)
    cos, sin = rope_tables(L)
    offsets = [int(o) for o in np.cumsum(SPLITS)[:-1]]
    for l in range(DEPTH):
        n = rmsnorm(h, norm_mix[l])
        parts = jnp.split(n @ w_in[l], offsets, axis=-1)
        c_q, c_kv, k_pe, gq, gk, gv, ga, gg, gate_a, gate_b = parts
        y_a = mla_branch(c_q, c_kv, k_pe, mla_q_norm[l], mla_w_uq[l], mla_kv_norm[l], mla_w_ukv[l], qn_nope[l], qn_pe[l], kn_nope[l], kn_pe[l], cos, sin, cid)
        y_b = gla_branch(gq, gk, gv, ga, gg, gla_w_a2[l], gla_b_a[l], gla_norm[l])
        mix = jax.nn.sigmoid(gate_a) * (y_a @ w_o_mla[l]) + jax.nn.sigmoid(gate_b) * (y_b @ w_o_gla[l])
        h = h + mix @ w_out[l]
        h = h + peer_ffn(rmsnorm(h, norm_ffn[l]), peer_w_q[l], peer_keys[l], peer_u[l], peer_v[l])
    return h[:, N_META:]
```

```python
import functools
import math

import jax
import jax.numpy as jnp
import numpy as np
from jax import lax
from jax.experimental import pallas as pl
from jax.experimental.pallas import tpu as pltpu

F32 = jnp.float32
BF16 = jnp.bfloat16
I32 = jnp.int32

D_MODEL = 1024
CHUNK = 64
N_META = 16
EPS = 1e-6

MLA_HEADS = 8
MLA_Q_RANK = 384
MLA_KV_RANK = 256
MLA_NOPE = 128
MLA_ROPE = 64
MLA_V = 128
ROPE_HALF = MLA_ROPE // 2
ROPE_THETA = 10000.0
MLA_QK_PAD = 256

GLA_HEADS = 4
GLA_DK = D_MODEL // 2
GLA_DV = D_MODEL
GLA_HK = GLA_DK // GLA_HEADS
GLA_HV = GLA_DV // GLA_HEADS
GLA_GATE_RANK = 16
GLA_TAU = 16.0

PEER_HEADS = 8
PEER_NKEYS = 128
PEER_EXPERTS = PEER_NKEYS * PEER_NKEYS
PEER_HALF = 128
PEER_TOPK = 16
PEER_PAIRS = PEER_HEADS * PEER_TOPK

LANES = 128
ROW_WORDS = 4
ROW_BF16 = 8
VMEM_LIMIT = 56 * 1024 * 1024

COL_CQ = 0
COL_CKV = 384
COL_KPE = 640
COL_GQ = 1024
COL_GK = 1536
COL_GV = 2048
COL_GG = 3072
COL_GATE_A = 4096
COL_GATE_B = 5120
PROJ_COLS = 6144


def _dot(a, b):
    return jnp.dot(a, b, preferred_element_type=F32)


def _dot_nt(a, b):
    return lax.dot_general(a, b, (((1,), (1,)), ((), ())), preferred_element_type=F32)


def _dot_tn(a, b):
    return lax.dot_general(a, b, (((0,), (0,)), ((), ())), preferred_element_type=F32)


def _rms(x, g):
    ms = jnp.mean(x * x, axis=-1, keepdims=True)
    return x * lax.rsqrt(ms + EPS) * g


def _split_bf16(x):
    hi = x.astype(BF16)
    lo = (x - hi.astype(F32)).astype(BF16)
    return hi, lo


def _inproj_kernel(x_ref, g_ref, w_ref, o_ref):
    n = _rms(x_ref[...], g_ref[...]).astype(BF16)
    o_ref[...] = _dot(n, w_ref[...]).astype(o_ref.dtype)


def _inproj(x2d, g, w, tm, tn):
    m, d = x2d.shape
    n = w.shape[1]
    return pl.pallas_call(
        _inproj_kernel,
        grid=(n // tn, m // tm),
        in_specs=[
            pl.BlockSpec((tm, d), lambda j, i: (i, 0)),
            pl.BlockSpec((1, d), lambda j, i: (0, 0)),
            pl.BlockSpec((d, tn), lambda j, i: (0, j)),
        ],
        out_specs=pl.BlockSpec((tm, tn), lambda j, i: (i, j)),
        out_shape=jax.ShapeDtypeStruct((m, n), BF16),
        compiler_params=pltpu.CompilerParams(vmem_limit_bytes=VMEM_LIMIT),
        name="inproj",
    )(x2d, g, w)


def _rope_pair_tile(xp, g_tile, cos, sin, lane):
    sq = xp * xp
    lo = lane < MLA_ROPE
    s_lo = jnp.sum(jnp.where(lo, sq, 0.0), axis=-1, keepdims=True)
    s_hi = jnp.sum(jnp.where(lo, 0.0, sq), axis=-1, keepdims=True)
    ms = jnp.where(lo, s_lo, s_hi) * (1.0 / MLA_ROPE)
    y = xp * lax.rsqrt(ms + EPS) * g_tile
    up = pltpu.roll(y, LANES - ROPE_HALF, axis=1)
    dn = pltpu.roll(y, ROPE_HALF, axis=1)
    partner = jnp.where((lane % MLA_ROPE) < ROPE_HALF, up, dn)
    return y * cos + partner * sin


def _prep_kernel(p_ref, cos_ref, sin_ref, qn_ref, wuq_ref, kvn_ref, wukv_ref, gqn_ref, gqp_ref,
                 gkn_ref, gkp_ref, wa2_ref, ba_ref, q_ref, k_ref, v_ref, la_ref):
    tm = p_ref.shape[0]
    lane = lax.broadcasted_iota(I32, (tm, LANES), 1)
    cos = cos_ref[...]
    sin = sin_ref[...]
    scale = (MLA_NOPE + MLA_ROPE) ** -0.5

    cq = p_ref[:, COL_CQ:COL_CQ + MLA_Q_RANK].astype(F32)
    qa = _dot(_rms(cq, qn_ref[...]).astype(BF16), wuq_ref[...])
    ckv = p_ref[:, COL_CKV:COL_CKV + MLA_KV_RANK].astype(F32)
    kva = _dot(_rms(ckv, kvn_ref[...]).astype(BF16), wukv_ref[...])

    misc = p_ref[:, COL_KPE:COL_KPE + LANES]
    kpe = _rope_pair_tile(misc.astype(F32), gkp_ref[...], cos, sin, lane)
    kpe = jnp.where(lane < MLA_ROPE, kpe, 0.0).astype(BF16)

    pe_off = MLA_HEADS * MLA_NOPE
    for j in range(MLA_HEADS // 2):
        t = _rope_pair_tile(qa[:, pe_off + j * LANES: pe_off + (j + 1) * LANES], gqp_ref[...], cos, sin, lane)
        t = t * scale
        even = jnp.where(lane < MLA_ROPE, t, 0.0)
        odd = jnp.where(lane < MLA_ROPE, pltpu.roll(t, MLA_ROPE, axis=1), 0.0)
        for h, pe in ((2 * j, even), (2 * j + 1, odd)):
            qn = _rms(qa[:, h * MLA_NOPE:(h + 1) * MLA_NOPE], gqn_ref[...]) * scale
            q_ref[:, h * MLA_QK_PAD: h * MLA_QK_PAD + MLA_NOPE] = qn.astype(BF16)
            q_ref[:, h * MLA_QK_PAD + MLA_NOPE:(h + 1) * MLA_QK_PAD] = pe.astype(BF16)
    for h in range(MLA_HEADS):
        kn = _rms(kva[:, h * MLA_NOPE:(h + 1) * MLA_NOPE], gkn_ref[...])
        k_ref[:, h * MLA_QK_PAD: h * MLA_QK_PAD + MLA_NOPE] = kn.astype(BF16)
        k_ref[:, h * MLA_QK_PAD + MLA_NOPE:(h + 1) * MLA_QK_PAD] = kpe
    v_ref[...] = kva[:, MLA_HEADS * MLA_NOPE:].astype(BF16)

    z = _dot(misc, wa2_ref[...]) + ba_ref[...]
    log_sig = jnp.minimum(z, 0.0) - jnp.log(1.0 + jnp.exp(-jnp.abs(z)))
    la_ref[...] = log_sig * (1.0 / GLA_TAU)


def _prep(proj, cos_t, sin_t, wts, tm, rows_per_seq):
    m = proj.shape[0]
    nblk = rows_per_seq // tm
    full = lambda a: pl.BlockSpec(a.shape, lambda i: (0,) * a.ndim)
    return pl.pallas_call(
        _prep_kernel,
        grid=(m // tm,),
        in_specs=[
            pl.BlockSpec((tm, 1024), lambda i: (i, 0)),
            pl.BlockSpec((tm, LANES), lambda i: (i % nblk, 0)),
            pl.BlockSpec((tm, LANES), lambda i: (i % nblk, 0)),
        ] + [full(w) for w in wts],
        out_specs=[
            pl.BlockSpec((tm, MLA_HEADS * MLA_QK_PAD), lambda i: (i, 0)),
            pl.BlockSpec((tm, MLA_HEADS * MLA_QK_PAD), lambda i: (i, 0)),
            pl.BlockSpec((tm, MLA_HEADS * MLA_V), lambda i: (i, 0)),
            pl.BlockSpec((tm, GLA_DK), lambda i: (i, 0)),
        ],
        out_shape=[
            jax.ShapeDtypeStruct((m, MLA_HEADS * MLA_QK_PAD), BF16),
            jax.ShapeDtypeStruct((m, MLA_HEADS * MLA_QK_PAD), BF16),
            jax.ShapeDtypeStruct((m, MLA_HEADS * MLA_V), BF16),
            jax.ShapeDtypeStruct((m, GLA_DK), F32),
        ],
        compiler_params=pltpu.CompilerParams(vmem_limit_bytes=VMEM_LIMIT),
        name="mla_gla_prep",
    )(proj, cos_t, sin_t, *wts)


def _attn_kernel(q_ref, k_ref, v_ref, km_ref, vm_ref, o_ref, *, tq):
    i = pl.program_id(1)
    row = lax.broadcasted_iota(I32, (tq, tq), 0)
    col = lax.broadcasted_iota(I32, (tq, tq), 1)
    diag_mask = (col // CHUNK) <= (row // CHUNK)
    meta_mask = lax.broadcasted_iota(I32, (tq, LANES), 1) < N_META

    for h in range(MLA_HEADS):
        qs = slice(h * MLA_QK_PAD, (h + 1) * MLA_QK_PAD)
        vs = slice(h * MLA_V, (h + 1) * MLA_V)
        q = q_ref[0, :, qs]

        s = jnp.where(meta_mask, _dot_nt(q, km_ref[:, qs]), -jnp.inf)
        m = jnp.max(s, axis=-1, keepdims=True)
        p = jnp.exp(s - m)
        l = jnp.sum(p, axis=-1, keepdims=True)
        acc = _dot(p.astype(BF16), vm_ref[:, vs])

        def update(carry, s, vblk):
            m, l, acc = carry
            m_new = jnp.maximum(m, jnp.max(s, axis=-1, keepdims=True))
            alpha = jnp.exp(m - m_new)
            p = jnp.exp(s - m_new)
            l = alpha * l + jnp.sum(p, axis=-1, keepdims=True)
            acc = alpha * acc + _dot(p.astype(BF16), vblk)
            return m_new, l, acc

        def body(j, carry):
            r0 = pl.multiple_of(j * tq, tq)
            s = _dot_nt(q, k_ref[0, pl.ds(r0, tq), qs])
            return update(carry, s, v_ref[0, pl.ds(r0, tq), vs])

        carry = lax.fori_loop(0, i, body, (m, l, acc))
        r0 = pl.multiple_of(i * tq, tq)
        s = jnp.where(diag_mask, _dot_nt(q, k_ref[0, pl.ds(r0, tq), qs]), -jnp.inf)
        m, l, acc = update(carry, s, v_ref[0, pl.ds(r0, tq), vs])
        o_ref[0, :, vs] = (acc / l).astype(o_ref.dtype)


def _attention(q, k, v, k_meta, v_meta, tq):
    b, r, _ = q.shape
    return pl.pallas_call(
        functools.partial(_attn_kernel, tq=tq),
        grid=(b, r // tq),
        in_specs=[
            pl.BlockSpec((1, tq, q.shape[2]), lambda bi, i: (bi, i, 0)),
            pl.BlockSpec((1, r, k.shape[2]), lambda bi, i: (bi, 0, 0)),
            pl.BlockSpec((1, r, v.shape[2]), lambda bi, i: (bi, 0, 0)),
            pl.BlockSpec(k_meta.shape, lambda bi, i: (0, 0)),
            pl.BlockSpec(v_meta.shape, lambda bi, i: (0, 0)),
        ],
        out_specs=pl.BlockSpec((1, tq, v.shape[2]), lambda bi, i: (bi, i, 0)),
        out_shape=jax.ShapeDtypeStruct((b, r, v.shape[2]), BF16),
        compiler_params=pltpu.CompilerParams(vmem_limit_bytes=VMEM_LIMIT),
        name="mla_attention",
    )(q, k, v, k_meta, v_meta)


def _gla_level_consts(c):
    idx = np.arange(c)
    tril = (idx[None, :] <= idx[:, None]).astype(np.float32)
    sels, masks = [], []
    m = c // 2
    while m >= 1:
        ref = (idx // (2 * m)) * (2 * m) + m - 1
        sels.append(tril[ref])
        same = (idx[:, None] // (2 * m)) == (idx[None, :] // (2 * m))
        masks.append(same & ((idx[:, None] % (2 * m)) >= m) & ((idx[None, :] % (2 * m)) < m))
        m //= 2
    return tril, np.stack(sels), np.stack(masks).astype(np.float32)


def _gla_kernel(gq_ref, gk_ref, gv_ref, gg_ref, la_ref, s0_ref, gn_ref, tril_ref, sel_ref, msk_ref,
                y_ref, sout_ref, s_scr, *, c, nlev):
    ci = pl.program_id(1)

    @pl.when(ci == 0)
    def _():
        s_scr[...] = s0_ref[...]

    hp = lax.Precision.HIGHEST
    la = la_ref[...]
    b_all = jnp.dot(tril_ref[...], la, precision=hp, preferred_element_type=F32)
    brefs = [jnp.dot(sel_ref[lv], la, precision=hp, preferred_element_type=F32) for lv in range(nlev)]
    eye = (lax.broadcasted_iota(I32, (c, c), 0) == lax.broadcasted_iota(I32, (c, c), 1))
    eye_k = (lax.broadcasted_iota(I32, (GLA_HK, GLA_HK), 0) == lax.broadcasted_iota(I32, (GLA_HK, GLA_HK), 1))

    for h in range(GLA_HEADS):
        ks = slice(h * GLA_HK, (h + 1) * GLA_HK)
        vs = slice(h * GLA_HV, (h + 1) * GLA_HV)
        q = gq_ref[:, ks].astype(F32) * (GLA_HK ** -0.5)
        k = gk_ref[:, ks].astype(F32)
        v = gv_ref[:, vs]
        bh = b_all[:, ks]
        s_old = s_scr[h]

        o = _dot((q * jnp.exp(bh)).astype(BF16), s_old.astype(BF16))
        a = jnp.where(eye, _dot_nt(q.astype(BF16), k.astype(BF16)), 0.0)
        for lv in range(nlev):
            br = brefs[lv][:, ks]
            qd = q * jnp.exp(jnp.minimum(bh - br, 0.0))
            kd = k * jnp.exp(jnp.minimum(br - bh, 0.0))
            a = a + msk_ref[lv] * _dot_nt(qd.astype(BF16), kd.astype(BF16))
        o = o + _dot(a.astype(BF16), v)

        b_last = bh[c - 1:c, :]
        kdec = k * jnp.exp(b_last - bh)
        dec_col = jnp.sum(jnp.where(eye_k, jnp.exp(b_last), 0.0), axis=1, keepdims=True)
        s_scr[h] = dec_col * s_old + _dot_tn(kdec.astype(BF16), v)

        gg = gg_ref[:, vs].astype(F32)
        silu = gg / (1.0 + jnp.exp(-gg))
        y_ref[:, vs] = (_rms(o, gn_ref[...]) * silu).astype(y_ref.dtype)

    @pl.when(ci == pl.num_programs(1) - 1)
    def _():
        sout_ref[0] = s_scr[...]


def _gla(proj, log_a, s0, g_norm, nb, c):
    m = proj.shape[0]
    nc = m // nb // c
    tril, sels, masks = _gla_level_consts(c)
    nlev = sels.shape[0]
    row = lambda bi, ci: bi * nc + ci
    return pl.pallas_call(
        functools.partial(_gla_kernel, c=c, nlev=nlev),
        grid=(nb, nc),
        in_specs=[
            pl.BlockSpec((c, GLA_DK), lambda bi, ci: (row(bi, ci), COL_GQ // GLA_DK)),
            pl.BlockSpec((c, GLA_DK), lambda bi, ci: (row(bi, ci), COL_GK // GLA_DK)),
            pl.BlockSpec((c, GLA_DV), lambda bi, ci: (row(bi, ci), COL_GV // GLA_DV)),
            pl.BlockSpec((c, GLA_DV), lambda bi, ci: (row(bi, ci), COL_GG // GLA_DV)),
            pl.BlockSpec((c, GLA_DK), lambda bi, ci: (row(bi, ci), 0)),
            pl.BlockSpec(s0.shape, lambda bi, ci: (0, 0, 0)),
            pl.BlockSpec(g_norm.shape, lambda bi, ci: (0, 0)),
            pl.BlockSpec(tril.shape, lambda bi, ci: (0, 0)),
            pl.BlockSpec(sels.shape, lambda bi, ci: (0, 0, 0)),
            pl.BlockSpec(masks.shape, lambda bi, ci: (0, 0, 0)),
        ],
        out_specs=[
            pl.BlockSpec((c, GLA_DV), lambda bi, ci: (row(bi, ci), 0)),
            pl.BlockSpec((1,) + s0.shape, lambda bi, ci: (bi, 0, 0, 0)),
        ],
        out_shape=[
            jax.ShapeDtypeStruct((m, GLA_DV), BF16),
            jax.ShapeDtypeStruct((nb,) + s0.shape, F32),
        ],
        scratch_shapes=[pltpu.VMEM(s0.shape, F32)],
        compiler_params=pltpu.CompilerParams(vmem_limit_bytes=VMEM_LIMIT),
        name="gla",
    )(proj, proj, proj, proj, log_a, s0, g_norm, jnp.asarray(tril), jnp.asarray(sels), jnp.asarray(masks))


def _mix_kernel(ya_ref, yb_ref, ga_ref, gb_ref, x_ref, wa_ref, wb_ref, wo_ref, gf_ref, wq_ref, keys_ref,
                h1_ref, hn_ref, st_ref):
    sig = lambda t: 1.0 / (1.0 + jnp.exp(-t))
    a = _dot(ya_ref[...], wa_ref[...])
    b = _dot(yb_ref[...], wb_ref[...])
    mix = sig(ga_ref[...].astype(F32)) * a + sig(gb_ref[...].astype(F32)) * b
    h1 = x_ref[...] + _dot(mix.astype(BF16), wo_ref[...])
    h1_ref[...] = h1
    hn = _rms(h1, gf_ref[...]).astype(BF16)
    hn_ref[...] = hn
    qp = _dot(hn, wq_ref[...]).astype(BF16)
    for hp in range(2 * PEER_HEADS):
        st_ref[hp] = _dot_nt(keys_ref[hp % 2], qp[:, hp * PEER_HALF:(hp + 1) * PEER_HALF])


def _mix(y_a, y_b, proj, x2d, w_a, w_b, w_o, g_ffn, w_q, keys, tm):
    m = x2d.shape[0]
    full = lambda a: pl.BlockSpec(a.shape, lambda i: (0,) * a.ndim)
    return pl.pallas_call(
        _mix_kernel,
        grid=(m // tm,),
        in_specs=[
            pl.BlockSpec((tm, D_MODEL), lambda i: (i, 0)),
            pl.BlockSpec((tm, D_MODEL), lambda i: (i, 0)),
            pl.BlockSpec((tm, D_MODEL), lambda i: (i, COL_GATE_A // D_MODEL)),
            pl.BlockSpec((tm, D_MODEL), lambda i: (i, COL_GATE_B // D_MODEL)),
            pl.BlockSpec((tm, D_MODEL), lambda i: (i, 0)),
            full(w_a), full(w_b), full(w_o), full(g_ffn), full(w_q), full(keys),
        ],
        out_specs=[
            pl.BlockSpec((tm, D_MODEL), lambda i: (i, 0)),
            pl.BlockSpec((tm, D_MODEL), lambda i: (i, 0)),
            pl.BlockSpec((2 * PEER_HEADS, PEER_NKEYS, tm), lambda i: (0, 0, i)),
        ],
        out_shape=[
            jax.ShapeDtypeStruct((m, D_MODEL), F32),
            jax.ShapeDtypeStruct((m, D_MODEL), BF16),
            jax.ShapeDtypeStruct((2 * PEER_HEADS, PEER_NKEYS, m), F32),
        ],
        compiler_params=pltpu.CompilerParams(vmem_limit_bytes=VMEM_LIMIT),
        name="mix_out_peer_scores",
    )(y_a, y_b, proj, proj, x2d, w_a, w_b, w_o, g_ffn, w_q, keys)


def _topk_rows(s, k, n):
    rid = lax.broadcasted_iota(I32, s.shape, 0)
    vals, ids = [], []
    for _ in range(k):
        m = jnp.max(s, axis=0, keepdims=True)
        am = jnp.min(jnp.where(s == m, rid, n), axis=0, keepdims=True)
        vals.append(m)
        ids.append(am)
        s = jnp.where(rid == am, -jnp.inf, s)
    return jnp.concatenate(vals, axis=0), jnp.concatenate(ids, axis=0)


def _topk_kernel(st_ref, idx_ref, gate_ref):
    def head(h, carry):
        v1, i1 = _topk_rows(st_ref[2 * h], PEER_TOPK, PEER_NKEYS)
        v2, i2 = _topk_rows(st_ref[2 * h + 1], PEER_TOPK, PEER_NKEYS)
        cs = jnp.concatenate([v1[a:a + 1] + v2 for a in range(PEER_TOPK)], axis=0)
        ci = jnp.concatenate([i1[a:a + 1] * PEER_NKEYS + i2 for a in range(PEER_TOPK)], axis=0)
        n = PEER_TOPK * PEER_TOPK
        pos = lax.broadcasted_iota(I32, cs.shape, 0)
        vals, ids = [], []
        for _ in range(PEER_TOPK):
            m = jnp.max(cs, axis=0, keepdims=True)
            ap = jnp.min(jnp.where(cs == m, pos, n), axis=0, keepdims=True)
            sel = pos == ap
            vals.append(m)
            ids.append(jnp.sum(jnp.where(sel, ci, 0), axis=0, keepdims=True))
            cs = jnp.where(sel, -jnp.inf, cs)
        best = jnp.concatenate(vals, axis=0)
        e = jnp.exp(best - best[0:1])
        gate_ref[h] = e / jnp.sum(e, axis=0, keepdims=True)
        idx_ref[h] = jnp.concatenate(ids, axis=0)
        return carry

    lax.fori_loop(0, PEER_HEADS, head, 0)


def _topk(st, tt):
    t = st.shape[2]
    return pl.pallas_call(
        _topk_kernel,
        grid=(t // tt,),
        in_specs=[pl.BlockSpec((2 * PEER_HEADS, PEER_NKEYS, tt), lambda i: (0, 0, i))],
        out_specs=[
            pl.BlockSpec((PEER_HEADS, PEER_TOPK, tt), lambda i: (0, 0, i)),
            pl.BlockSpec((PEER_HEADS, PEER_TOPK, tt), lambda i: (0, 0, i)),
        ],
        out_shape=[
            jax.ShapeDtypeStruct((PEER_HEADS, PEER_TOPK, t), I32),
            jax.ShapeDtypeStruct((PEER_HEADS, PEER_TOPK, t), F32),
        ],
        compiler_params=pltpu.CompilerParams(vmem_limit_bytes=VMEM_LIMIT),
        name="peer_topk",
    )(st)


def _pack_rows(a):
    n = a.shape[0]
    x = a.astype(BF16).reshape(n, ROW_WORDS, 2, LANES).transpose(0, 1, 3, 2)
    return lax.bitcast_convert_type(x, I32).reshape(n * ROW_WORDS, LANES)


def _lane_group_consts():
    lane = np.arange(PEER_PAIRS * ROW_BF16)
    diag = (lane[None, :] % ROW_BF16) == np.arange(ROW_BF16)[:, None]
    group = (lane[:, None] // ROW_BF16) == np.arange(PEER_PAIRS)[None, :]
    return jnp.asarray(diag, F32), jnp.asarray(group, BF16), jnp.asarray(group.T, BF16)


def _idx_copy(idx_hbm, idx_smem, sem, step, slot, n):
    return pltpu.make_async_copy(idx_hbm.at[pl.ds(step * n, n)], idx_smem.at[pl.ds(slot * n, n)], sem.at[slot])


def _load_table_and_indices(tab_hbm, idx_hbm, tab_v, idx_s, tab_sem, idx_sem, n):
    i = pl.program_id(0)
    slot = i % 2

    @pl.when(i == 0)
    def _():
        table = pltpu.make_async_copy(tab_hbm, tab_v, tab_sem)
        table.start()
        _idx_copy(idx_hbm, idx_s, idx_sem, 0, 0, n).start()
        table.wait()

    _idx_copy(idx_hbm, idx_s, idx_sem, i, slot, n).wait()

    @pl.when(i + 1 < pl.num_programs(0))
    def _():
        _idx_copy(idx_hbm, idx_s, idx_sem, i + 1, 1 - slot, n).start()

    return slot


def _gather_rows(tab_v, idx_s, slot, t, tt, stage):
    base = (slot * tt + t) * PEER_PAIRS
    for p in range(PEER_PAIRS):
        e = pl.multiple_of(idx_s[base + p], ROW_WORDS)
        stage[p * ROW_WORDS:(p + 1) * ROW_WORDS, :] = tab_v[pl.ds(e, ROW_WORDS), :]
    return pltpu.bitcast(stage[...], BF16)


def _erf(x):
    return lax.erf(x)


def _peer_u_kernel(idx_hbm, tab_hbm, hp_ref, gate_ref, diag_ref, group_ref, coef_ref,
                   tab_v, idx_s, stage, act, tab_sem, idx_sem, *, tt):
    slot = _load_table_and_indices(tab_hbm, idx_hbm, tab_v, idx_s, tab_sem, idx_sem, tt * PEER_PAIRS)
    diag = diag_ref[...]
    group = group_ref[...]

    def token(t, carry):
        g = _gather_rows(tab_v, idx_s, slot, t, tt, stage)
        ht = pltpu.bitcast(hp_ref[pl.ds(pl.multiple_of(t * ROW_WORDS, ROW_WORDS), ROW_WORDS), :], BF16)
        z = _dot_nt(ht, g) * diag
        hi, lo = _split_bf16(z)
        a8 = _dot(hi, group) + _dot(lo, group)
        act[pl.ds(t, 1), :] = jnp.sum(a8, axis=0, keepdims=True)
        return carry

    lax.fori_loop(0, tt, token, 0)
    a = act[...]
    gelu = 0.5 * a * (1.0 + _erf(a * (2.0 ** -0.5)))
    coef_ref[...] = gelu * gate_ref[...]


def _peer_v_kernel(idx_hbm, tab_hbm, coef_ref, h1_ref, diag_ref, group_t_ref, y_ref,
                   tab_v, idx_s, stage, tab_sem, idx_sem, *, tt):
    slot = _load_table_and_indices(tab_hbm, idx_hbm, tab_v, idx_s, tab_sem, idx_sem, tt * PEER_PAIRS)
    diag = diag_ref[...]
    group_t = group_t_ref[...]

    def token(t, carry):
        g = _gather_rows(tab_v, idx_s, slot, t, tt, stage)
        c = jnp.broadcast_to(coef_ref[pl.ds(t, 1), :], (ROW_BF16, PEER_PAIRS))
        hi, lo = _split_bf16(c)
        crep = (_dot(hi, group_t) + _dot(lo, group_t)) * diag
        hi, lo = _split_bf16(crep)
        out = _dot(hi, g) + _dot(lo, g)
        rows = pl.ds(pl.multiple_of(t * ROW_BF16, ROW_BF16), ROW_BF16)
        y_ref[rows, :] = h1_ref[rows, :] + out
        return carry

    lax.fori_loop(0, tt, token, 0)


def _peer_scratch(tt, with_act):
    shapes = [
        pltpu.VMEM((PEER_EXPERTS * ROW_WORDS, LANES), I32),
        pltpu.SMEM((2 * tt * PEER_PAIRS,), I32),
        pltpu.VMEM((PEER_PAIRS * ROW_WORDS, LANES), I32),
    ]
    if with_act:
        shapes.append(pltpu.VMEM((tt, PEER_PAIRS), F32))
    return shapes + [pltpu.SemaphoreType.DMA, pltpu.SemaphoreType.DMA((2,))]


def _peer_u(idx_flat, tab, hn_packed, gate, tt):
    t = gate.shape[0]
    diag, group, _ = _lane_group_consts()
    any_spec = pl.BlockSpec(memory_space=pl.ANY)
    return pl.pallas_call(
        functools.partial(_peer_u_kernel, tt=tt),
        grid=(t // tt,),
        in_specs=[
            any_spec, any_spec,
            pl.BlockSpec((tt * ROW_WORDS, LANES), lambda i: (i, 0)),
            pl.BlockSpec((tt, PEER_PAIRS), lambda i: (i, 0)),
            pl.BlockSpec(diag.shape, lambda i: (0, 0)),
            pl.BlockSpec(group.shape, lambda i: (0, 0)),
        ],
        out_specs=pl.BlockSpec((tt, PEER_PAIRS), lambda i: (i, 0)),
        out_shape=jax.ShapeDtypeStruct((t, PEER_PAIRS), F32),
        scratch_shapes=_peer_scratch(tt, True),
        compiler_params=pltpu.CompilerParams(vmem_limit_bytes=VMEM_LIMIT),
        name="peer_u",
    )(idx_flat, tab, hn_packed, gate, diag, group)


def _peer_v(idx_flat, tab, coef, h1_rows, tt):
    t = coef.shape[0]
    diag, _, group_t = _lane_group_consts()
    any_spec = pl.BlockSpec(memory_space=pl.ANY)
    return pl.pallas_call(
        functools.partial(_peer_v_kernel, tt=tt),
        grid=(t // tt,),
        in_specs=[
            any_spec, any_spec,
            pl.BlockSpec((tt, PEER_PAIRS), lambda i: (i, 0)),
            pl.BlockSpec((tt * ROW_BF16, LANES), lambda i: (i, 0)),
            pl.BlockSpec(diag.shape, lambda i: (0, 0)),
            pl.BlockSpec(group_t.shape, lambda i: (0, 0)),
        ],
        out_specs=pl.BlockSpec((tt * ROW_BF16, LANES), lambda i: (i, 0)),
        out_shape=jax.ShapeDtypeStruct((t * ROW_BF16, LANES), F32),
        scratch_shapes=_peer_scratch(tt, False),
        compiler_params=pltpu.CompilerParams(vmem_limit_bytes=VMEM_LIMIT),
        name="peer_v",
    )(idx_flat, tab, coef, h1_rows, diag, group_t)


def _rope_tables(pos):
    inv_freq = ROPE_THETA ** (-jnp.arange(ROPE_HALF, dtype=F32) / ROPE_HALF)
    ang = pos.astype(F32)[:, None] * inv_freq[None, :]
    cos, sin = jnp.cos(ang), jnp.sin(ang)
    cos_t = jnp.tile(cos, (1, LANES // ROPE_HALF))
    sin_t = jnp.tile(jnp.concatenate([-sin, sin], axis=1), (1, LANES // MLA_ROPE))
    return cos_t, sin_t


def _pack_in_proj(w_in):
    offs = np.cumsum([0, MLA_Q_RANK, MLA_KV_RANK, MLA_ROPE, GLA_DK, GLA_DK, GLA_DV, GLA_GATE_RANK, GLA_DV,
                      D_MODEL, D_MODEL])
    part = lambda i: w_in[:, int(offs[i]):int(offs[i + 1])]
    zeros = lambda n: jnp.zeros((D_MODEL, n), w_in.dtype)
    cols = [part(0), part(1), part(2), part(6), zeros(LANES - MLA_ROPE - GLA_GATE_RANK), zeros(COL_GQ - COL_KPE - LANES),
            part(3), part(4), part(5), part(7), part(8), part(9)]
    w = jnp.concatenate(cols, axis=1).astype(BF16)
    assert w.shape[1] == PROJ_COLS
    return w


def _head_major(w, widths):
    per = sum(widths)
    w = w.reshape(w.shape[0], -1, per)
    parts, o = [], 0
    for wd in widths:
        parts.append(w[:, :, o:o + wd].reshape(w.shape[0], -1))
        o += wd
    return jnp.concatenate(parts, axis=1)


def _row_tile(m, want):
    t = min(m, want)
    assert m % t == 0
    return t


def kernel(x, meta, norm_mix, w_in, mla_q_norm, mla_w_uq, mla_kv_norm, mla_w_ukv, qn_nope, qn_pe, kn_nope, kn_pe, gla_w_a2, gla_b_a, gla_norm, w_o_mla, w_o_gla, w_out, norm_ffn, peer_w_q, peer_keys, peer_u, peer_v):
    nb, r, d = x.shape
    assert d == D_MODEL and norm_mix.shape[0] == 1 and meta.shape == (N_META, D_MODEL)
    assert r % 256 == 0
    t = nb * r
    x2d = x.reshape(t, d)
    row = lambda a: a.reshape(1, -1).astype(F32)

    w1 = _pack_in_proj(w_in[0])
    g_mix = row(norm_mix[0])
    tile2 = lambda g: jnp.tile(row(g), (1, LANES // MLA_ROPE))
    gkp = jnp.concatenate([row(kn_pe[0]), jnp.ones((1, LANES - MLA_ROPE), F32)], axis=1)
    wa2 = jnp.zeros((LANES, GLA_DK), F32).at[MLA_ROPE:MLA_ROPE + GLA_GATE_RANK].set(gla_w_a2[0]).astype(BF16)
    prep_w = (
        row(mla_q_norm[0]), _head_major(mla_w_uq[0], (MLA_NOPE, MLA_ROPE)).astype(BF16),
        row(mla_kv_norm[0]), _head_major(mla_w_ukv[0], (MLA_NOPE, MLA_V)).astype(BF16),
        row(qn_nope[0]), tile2(qn_pe[0]), row(kn_nope[0]), gkp, wa2, row(gla_b_a[0]),
    )
    g_gla = row(gla_norm[0])

    proj_m = _inproj(meta.astype(F32), g_mix, w1, N_META, PROJ_COLS // 2)
    cos_m, sin_m = _rope_tables(jnp.arange(N_META))
    _, k_m, v_m, la_m = _prep(proj_m, cos_m, sin_m, prep_w, N_META, N_META)
    s_zero = jnp.zeros((GLA_HEADS, GLA_HK, GLA_HV), F32)
    _, s_meta = _gla(proj_m, la_m, s_zero, g_gla, 1, N_META)
    pad = ((0, LANES - N_META), (0, 0))
    k_m, v_m = jnp.pad(k_m, pad), jnp.pad(v_m, pad)

    proj = _inproj(x2d, g_mix, w1, _row_tile(t, 512), PROJ_COLS // 2)
    cos_t, sin_t = _rope_tables(N_META + jnp.arange(r))
    q, k, v, log_a = _prep(proj, cos_t, sin_t, prep_w, 256, r)
    y_a = _attention(q.reshape(nb, r, -1), k.reshape(nb, r, -1), v.reshape(nb, r, -1), k_m, v_m, 256)
    y_b, _ = _gla(proj, log_a, s_meta[0], g_gla, nb, CHUNK)

    h1, hn, st = _mix(y_a.reshape(t, -1), y_b, proj, x2d, w_o_mla[0].astype(BF16), w_o_gla[0].astype(BF16),
                      w_out[0].astype(BF16), row(norm_ffn[0]), peer_w_q[0].astype(BF16),
                      peer_keys[0].astype(BF16), _row_tile(t, 512))
    idx, gate = _topk(st, 256)
    idx = (idx.reshape(PEER_PAIRS, t).T * ROW_WORDS).reshape(-1)
    gate = gate.reshape(PEER_PAIRS, t).T

    tt = 128
    coef = _peer_u(idx, _pack_rows(peer_u[0]), _pack_rows(hn), gate, tt)
    y = _peer_v(idx, _pack_rows(peer_v[0]), coef, h1.reshape(t * ROW_BF16, LANES), tt)
    return y.reshape(nb, r, d)
```

```python
import functools
import math

import jax
import jax.numpy as jnp
import numpy as np
from jax import lax
from jax.experimental import pallas as pl
from jax.experimental.pallas import tpu as pltpu

F32 = jnp.float32
BF16 = jnp.bfloat16
I32 = jnp.int32

D_MODEL = 1024
CHUNK = 64
N_META = 16
EPS = 1e-6

MLA_HEADS = 8
MLA_Q_RANK = 384
MLA_KV_RANK = 256
MLA_NOPE = 128
MLA_ROPE = 64
MLA_V = 128
ROPE_HALF = MLA_ROPE // 2
ROPE_THETA = 10000.0
MLA_QK_PAD = 256

GLA_HEADS = 4
GLA_DK = D_MODEL // 2
GLA_DV = D_MODEL
GLA_HK = GLA_DK // GLA_HEADS
GLA_HV = GLA_DV // GLA_HEADS
GLA_GATE_RANK = 16
GLA_TAU = 16.0

PEER_HEADS = 8
PEER_NKEYS = 128
PEER_EXPERTS = PEER_NKEYS * PEER_NKEYS
PEER_HALF = 128
PEER_TOPK = 16
PEER_PAIRS = PEER_HEADS * PEER_TOPK

LANES = 128
ROW_WORDS = 4
ROW_BF16 = 8
VMEM_LIMIT = 56 * 1024 * 1024

COL_CQ = 0
COL_CKV = 384
COL_KPE = 640
COL_GQ = 1024
COL_GK = 1536
COL_GV = 2048
COL_GG = 3072
COL_GATE_A = 4096
COL_GATE_B = 5120
PROJ_COLS = 6144


def _dot(a, b):
    return jnp.dot(a, b, preferred_element_type=F32)


def _dot_nt(a, b):
    return lax.dot_general(a, b, (((1,), (1,)), ((), ())), preferred_element_type=F32)


def _dot_tn(a, b):
    return lax.dot_general(a, b, (((0,), (0,)), ((), ())), preferred_element_type=F32)


def _rms(x, g):
    ms = jnp.mean(x * x, axis=-1, keepdims=True)
    return x * lax.rsqrt(ms + EPS) * g


def _split_bf16(x):
    hi = x.astype(BF16)
    lo = (x - hi.astype(F32)).astype(BF16)
    return hi, lo


def _inproj_kernel(x_ref, g_ref, w_ref, o_ref):
    n = _rms(x_ref[...], g_ref[...]).astype(BF16)
    o_ref[...] = _dot(n, w_ref[...]).astype(o_ref.dtype)


def _inproj(x2d, g, w, tm, tn):
    m, d = x2d.shape
    n = w.shape[1]
    return pl.pallas_call(
        _inproj_kernel,
        grid=(n // tn, m // tm),
        in_specs=[
            pl.BlockSpec((tm, d), lambda j, i: (i, 0)),
            pl.BlockSpec((1, d), lambda j, i: (0, 0)),
            pl.BlockSpec((d, tn), lambda j, i: (0, j)),
        ],
        out_specs=pl.BlockSpec((tm, tn), lambda j, i: (i, j)),
        out_shape=jax.ShapeDtypeStruct((m, n), BF16),
        compiler_params=pltpu.CompilerParams(vmem_limit_bytes=VMEM_LIMIT),
        name="inproj",
    )(x2d, g, w)


def _rope_pair_tile(xp, g_tile, cos, sin, lane):
    sq = xp * xp
    lo = lane < MLA_ROPE
    s_lo = jnp.sum(jnp.where(lo, sq, 0.0), axis=-1, keepdims=True)
    s_hi = jnp.sum(jnp.where(lo, 0.0, sq), axis=-1, keepdims=True)
    ms = jnp.where(lo, s_lo, s_hi) * (1.0 / MLA_ROPE)
    y = xp * lax.rsqrt(ms + EPS) * g_tile
    up = pltpu.roll(y, LANES - ROPE_HALF, axis=1)
    dn = pltpu.roll(y, ROPE_HALF, axis=1)
    partner = jnp.where((lane % MLA_ROPE) < ROPE_HALF, up, dn)
    return y * cos + partner * sin


def _prep_kernel(p_ref, cos_ref, sin_ref, qn_ref, wuq_ref, kvn_ref, wukv_ref, gqn_ref, gqp_ref,
                 gkn_ref, gkp_ref, wa2_ref, ba_ref, q_ref, k_ref, v_ref, la_ref):
    tm = p_ref.shape[0]
    lane = lax.broadcasted_iota(I32, (tm, LANES), 1)
    cos = cos_ref[...]
    sin = sin_ref[...]
    scale = (MLA_NOPE + MLA_ROPE) ** -0.5

    cq = p_ref[:, COL_CQ:COL_CQ + MLA_Q_RANK].astype(F32)
    qa = _dot(_rms(cq, qn_ref[...]).astype(BF16), wuq_ref[...])
    ckv = p_ref[:, COL_CKV:COL_CKV + MLA_KV_RANK].astype(F32)
    kva = _dot(_rms(ckv, kvn_ref[...]).astype(BF16), wukv_ref[...])

    misc = p_ref[:, COL_KPE:COL_KPE + LANES]
    kpe = _rope_pair_tile(misc.astype(F32), gkp_ref[...], cos, sin, lane)
    kpe = jnp.where(lane < MLA_ROPE, kpe, 0.0).astype(BF16)

    pe_off = MLA_HEADS * MLA_NOPE
    for j in range(MLA_HEADS // 2):
        t = _rope_pair_tile(qa[:, pe_off + j * LANES: pe_off + (j + 1) * LANES], gqp_ref[...], cos, sin, lane)
        t = t * scale
        even = jnp.where(lane < MLA_ROPE, t, 0.0)
        odd = jnp.where(lane < MLA_ROPE, pltpu.roll(t, MLA_ROPE, axis=1), 0.0)
        for h, pe in ((2 * j, even), (2 * j + 1, odd)):
            qn = _rms(qa[:, h * MLA_NOPE:(h + 1) * MLA_NOPE], gqn_ref[...]) * scale
            q_ref[:, h * MLA_QK_PAD: h * MLA_QK_PAD + MLA_NOPE] = qn.astype(BF16)
            q_ref[:, h * MLA_QK_PAD + MLA_NOPE:(h + 1) * MLA_QK_PAD] = pe.astype(BF16)
    for h in range(MLA_HEADS):
        kn = _rms(kva[:, h * MLA_NOPE:(h + 1) * MLA_NOPE], gkn_ref[...])
        k_ref[:, h * MLA_QK_PAD: h * MLA_QK_PAD + MLA_NOPE] = kn.astype(BF16)
        k_ref[:, h * MLA_QK_PAD + MLA_NOPE:(h + 1) * MLA_QK_PAD] = kpe
    v_ref[...] = kva[:, MLA_HEADS * MLA_NOPE:].astype(BF16)

    z = _dot(misc, wa2_ref[...]) + ba_ref[...]
    log_sig = jnp.minimum(z, 0.0) - jnp.log(1.0 + jnp.exp(-jnp.abs(z)))
    la_ref[...] = log_sig * (1.0 / GLA_TAU)


def _prep(proj, cos_t, sin_t, wts, tm, rows_per_seq):
    m = proj.shape[0]
    nblk = rows_per_seq // tm
    full = lambda a: pl.BlockSpec(a.shape, lambda i: (0,) * a.ndim)
    return pl.pallas_call(
        _prep_kernel,
        grid=(m // tm,),
        in_specs=[
            pl.BlockSpec((tm, 1024), lambda i: (i, 0)),
            pl.BlockSpec((tm, LANES), lambda i: (i % nblk, 0)),
            pl.BlockSpec((tm, LANES), lambda i: (i % nblk, 0)),
        ] + [full(w) for w in wts],
        out_specs=[
            pl.BlockSpec((tm, MLA_HEADS * MLA_QK_PAD), lambda i: (i, 0)),
            pl.BlockSpec((tm, MLA_HEADS * MLA_QK_PAD), lambda i: (i, 0)),
            pl.BlockSpec((tm, MLA_HEADS * MLA_V), lambda i: (i, 0)),
            pl.BlockSpec((tm, GLA_DK), lambda i: (i, 0)),
        ],
        out_shape=[
            jax.ShapeDtypeStruct((m, MLA_HEADS * MLA_QK_PAD), BF16),
            jax.ShapeDtypeStruct((m, MLA_HEADS * MLA_QK_PAD), BF16),
            jax.ShapeDtypeStruct((m, MLA_HEADS * MLA_V), BF16),
            jax.ShapeDtypeStruct((m, GLA_DK), F32),
        ],
        compiler_params=pltpu.CompilerParams(vmem_limit_bytes=VMEM_LIMIT),
        name="mla_gla_prep",
    )(proj, cos_t, sin_t, *wts)


def _attn_kernel(q_ref, k_ref, v_ref, km_ref, vm_ref, o_ref, *, tq):
    i = pl.program_id(1)
    row = lax.broadcasted_iota(I32, (tq, tq), 0)
    col = lax.broadcasted_iota(I32, (tq, tq), 1)
    diag_mask = (col // CHUNK) <= (row // CHUNK)
    meta_mask = lax.broadcasted_iota(I32, (tq, LANES), 1) < N_META

    for h in range(MLA_HEADS):
        qs = slice(h * MLA_QK_PAD, (h + 1) * MLA_QK_PAD)
        vs = slice(h * MLA_V, (h + 1) * MLA_V)
        q = q_ref[0, :, qs]

        s = jnp.where(meta_mask, _dot_nt(q, km_ref[:, qs]), -jnp.inf)
        m = jnp.max(s, axis=-1, keepdims=True)
        p = jnp.exp(s - m)
        l = jnp.sum(p, axis=-1, keepdims=True)
        acc = _dot(p.astype(BF16), vm_ref[:, vs])

        def update(carry, s, vblk):
            m, l, acc = carry
            m_new = jnp.maximum(m, jnp.max(s, axis=-1, keepdims=True))
            alpha = jnp.exp(m - m_new)
            p = jnp.exp(s - m_new)
            l = alpha * l + jnp.sum(p, axis=-1, keepdims=True)
            acc = alpha * acc + _dot(p.astype(BF16), vblk)
            return m_new, l, acc

        def body(j, carry):
            r0 = pl.multiple_of(j * tq, tq)
            s = _dot_nt(q, k_ref[0, pl.ds(r0, tq), qs])
            return update(carry, s, v_ref[0, pl.ds(r0, tq), vs])

        carry = lax.fori_loop(0, i, body, (m, l, acc))
        r0 = pl.multiple_of(i * tq, tq)
        s = jnp.where(diag_mask, _dot_nt(q, k_ref[0, pl.ds(r0, tq), qs]), -jnp.inf)
        m, l, acc = update(carry, s, v_ref[0, pl.ds(r0, tq), vs])
        o_ref[0, :, vs] = (acc / l).astype(o_ref.dtype)


def _attention(q, k, v, k_meta, v_meta, tq):
    b, r, _ = q.shape
    return pl.pallas_call(
        functools.partial(_attn_kernel, tq=tq),
        grid=(b, r // tq),
        in_specs=[
            pl.BlockSpec((1, tq, q.shape[2]), lambda bi, i: (bi, i, 0)),
            pl.BlockSpec((1, r, k.shape[2]), lambda bi, i: (bi, 0, 0)),
            pl.BlockSpec((1, r, v.shape[2]), lambda bi, i: (bi, 0, 0)),
            pl.BlockSpec(k_meta.shape, lambda bi, i: (0, 0)),
            pl.BlockSpec(v_meta.shape, lambda bi, i: (0, 0)),
        ],
        out_specs=pl.BlockSpec((1, tq, v.shape[2]), lambda bi, i: (bi, i, 0)),
        out_shape=jax.ShapeDtypeStruct((b, r, v.shape[2]), BF16),
        compiler_params=pltpu.CompilerParams(vmem_limit_bytes=VMEM_LIMIT),
        name="mla_attention",
    )(q, k, v, k_meta, v_meta)


def _gla_level_consts(c):
    idx = np.arange(c)
    tril = (idx[None, :] <= idx[:, None]).astype(np.float32)
    sels, masks = [], []
    m = c // 2
    while m >= 1:
        ref = (idx // (2 * m)) * (2 * m) + m - 1
        sels.append(tril[ref])
        same = (idx[:, None] // (2 * m)) == (idx[None, :] // (2 * m))
        masks.append(same & ((idx[:, None] % (2 * m)) >= m) & ((idx[None, :] % (2 * m)) < m))
        m //= 2
    return tril, np.stack(sels), np.stack(masks).astype(np.float32)


def _gla_kernel(gq_ref, gk_ref, gv_ref, gg_ref, la_ref, s0_ref, gn_ref, tril_ref, sel_ref, msk_ref,
                y_ref, sout_ref, s_scr, *, c, nlev):
    ci = pl.program_id(1)

    @pl.when(ci == 0)
    def _():
        s_scr[...] = s0_ref[...]

    hp = lax.Precision.HIGHEST
    la = la_ref[...]
    b_all = jnp.dot(tril_ref[...], la, precision=hp, preferred_element_type=F32)
    brefs = [jnp.dot(sel_ref[lv], la, precision=hp, preferred_element_type=F32) for lv in range(nlev)]
    eye = (lax.broadcasted_iota(I32, (c, c), 0) == lax.broadcasted_iota(I32, (c, c), 1))
    eye_k = (lax.broadcasted_iota(I32, (GLA_HK, GLA_HK), 0) == lax.broadcasted_iota(I32, (GLA_HK, GLA_HK), 1))

    for h in range(GLA_HEADS):
        ks = slice(h * GLA_HK, (h + 1) * GLA_HK)
        vs = slice(h * GLA_HV, (h + 1) * GLA_HV)
        q = gq_ref[:, ks].astype(F32) * (GLA_HK ** -0.5)
        k = gk_ref[:, ks].astype(F32)
        v = gv_ref[:, vs]
        bh = b_all[:, ks]
        s_old = s_scr[h]

        o = _dot((q * jnp.exp(bh)).astype(BF16), s_old.astype(BF16))
        a = jnp.where(eye, _dot_nt(q.astype(BF16), k.astype(BF16)), 0.0)
        for lv in range(nlev):
            br = brefs[lv][:, ks]
            qd = q * jnp.exp(jnp.minimum(bh - br, 0.0))
            kd = k * jnp.exp(jnp.minimum(br - bh, 0.0))
            a = a + msk_ref[lv] * _dot_nt(qd.astype(BF16), kd.astype(BF16))
        o = o + _dot(a.astype(BF16), v)

        b_last = bh[c - 1:c, :]
        kdec = k * jnp.exp(b_last - bh)
        dec_col = jnp.sum(jnp.where(eye_k, jnp.exp(b_last), 0.0), axis=1, keepdims=True)
        s_scr[h] = dec_col * s_old + _dot_tn(kdec.astype(BF16), v)

        gg = gg_ref[:, vs].astype(F32)
        silu = gg / (1.0 + jnp.exp(-gg))
        y_ref[:, vs] = (_rms(o, gn_ref[...]) * silu).astype(y_ref.dtype)

    @pl.when(ci == pl.num_programs(1) - 1)
    def _():
        sout_ref[0] = s_scr[...]


def _gla(proj, log_a, s0, g_norm, nb, c):
    m = proj.shape[0]
    nc = m // nb // c
    tril, sels, masks = _gla_level_consts(c)
    nlev = sels.shape[0]
    row = lambda bi, ci: bi * nc + ci
    return pl.pallas_call(
        functools.partial(_gla_kernel, c=c, nlev=nlev),
        grid=(nb, nc),
        in_specs=[
            pl.BlockSpec((c, GLA_DK), lambda bi, ci: (row(bi, ci), COL_GQ // GLA_DK)),
            pl.BlockSpec((c, GLA_DK), lambda bi, ci: (row(bi, ci), COL_GK // GLA_DK)),
            pl.BlockSpec((c, GLA_DV), lambda bi, ci: (row(bi, ci), COL_GV // GLA_DV)),
            pl.BlockSpec((c, GLA_DV), lambda bi, ci: (row(bi, ci), COL_GG // GLA_DV)),
            pl.BlockSpec((c, GLA_DK), lambda bi, ci: (row(bi, ci), 0)),
            pl.BlockSpec(s0.shape, lambda bi, ci: (0, 0, 0)),
            pl.BlockSpec(g_norm.shape, lambda bi, ci: (0, 0)),
            pl.BlockSpec(tril.shape, lambda bi, ci: (0, 0)),
            pl.BlockSpec(sels.shape, lambda bi, ci: (0, 0, 0)),
            pl.BlockSpec(masks.shape, lambda bi, ci: (0, 0, 0)),
        ],
        out_specs=[
            pl.BlockSpec((c, GLA_DV), lambda bi, ci: (row(bi, ci), 0)),
            pl.BlockSpec((1,) + s0.shape, lambda bi, ci: (bi, 0, 0, 0)),
        ],
        out_shape=[
            jax.ShapeDtypeStruct((m, GLA_DV), BF16),
            jax.ShapeDtypeStruct((nb,) + s0.shape, F32),
        ],
        scratch_shapes=[pltpu.VMEM(s0.shape, F32)],
        compiler_params=pltpu.CompilerParams(vmem_limit_bytes=VMEM_LIMIT),
        name="gla",
    )(proj, proj, proj, proj, log_a, s0, g_norm, jnp.asarray(tril), jnp.asarray(sels), jnp.asarray(masks))


def _mix_kernel(ya_ref, yb_ref, ga_ref, gb_ref, x_ref, wa_ref, wb_ref, wo_ref, gf_ref, wq_ref, keys_ref,
                h1_ref, hn_ref, st_ref):
    sig = lambda t: 1.0 / (1.0 + jnp.exp(-t))
    a = _dot(ya_ref[...], wa_ref[...])
    b = _dot(yb_ref[...], wb_ref[...])
    mix = sig(ga_ref[...].astype(F32)) * a + sig(gb_ref[...].astype(F32)) * b
    h1 = x_ref[...] + _dot(mix.astype(BF16), wo_ref[...])
    h1_ref[...] = h1
    hn = _rms(h1, gf_ref[...]).astype(BF16)
    hn_ref[...] = hn
    qp = _dot(hn, wq_ref[...]).astype(BF16)
    for hp in range(2 * PEER_HEADS):
        st_ref[hp] = _dot_nt(keys_ref[hp % 2], qp[:, hp * PEER_HALF:(hp + 1) * PEER_HALF])


def _mix(y_a, y_b, proj, x2d, w_a, w_b, w_o, g_ffn, w_q, keys, tm):
    m = x2d.shape[0]
    full = lambda a: pl.BlockSpec(a.shape, lambda i: (0,) * a.ndim)
    return pl.pallas_call(
        _mix_kernel,
        grid=(m // tm,),
        in_specs=[
            pl.BlockSpec((tm, D_MODEL), lambda i: (i, 0)),
            pl.BlockSpec((tm, D_MODEL), lambda i: (i, 0)),
            pl.BlockSpec((tm, D_MODEL), lambda i: (i, COL_GATE_A // D_MODEL)),
            pl.BlockSpec((tm, D_MODEL), lambda i: (i, COL_GATE_B // D_MODEL)),
            pl.BlockSpec((tm, D_MODEL), lambda i: (i, 0)),
            full(w_a), full(w_b), full(w_o), full(g_ffn), full(w_q), full(keys),
        ],
        out_specs=[
            pl.BlockSpec((tm, D_MODEL), lambda i: (i, 0)),
            pl.BlockSpec((tm, D_MODEL), lambda i: (i, 0)),
            pl.BlockSpec((2 * PEER_HEADS, PEER_NKEYS, tm), lambda i: (0, 0, i)),
        ],
        out_shape=[
            jax.ShapeDtypeStruct((m, D_MODEL), F32),
            jax.ShapeDtypeStruct((m, D_MODEL), BF16),
            jax.ShapeDtypeStruct((2 * PEER_HEADS, PEER_NKEYS, m), F32),
        ],
        compiler_params=pltpu.CompilerParams(vmem_limit_bytes=VMEM_LIMIT),
        name="mix_out_peer_scores",
    )(y_a, y_b, proj, proj, x2d, w_a, w_b, w_o, g_ffn, w_q, keys)


def _topk_rows(s, k, n):
    rid = lax.broadcasted_iota(I32, s.shape, 0)
    vals, ids = [], []
    for _ in range(k):
        m = jnp.max(s, axis=0, keepdims=True)
        am = jnp.min(jnp.where(s == m, rid, n), axis=0, keepdims=True)
        vals.append(m)
        ids.append(am)
        s = jnp.where(rid == am, -jnp.inf, s)
    return jnp.concatenate(vals, axis=0), jnp.concatenate(ids, axis=0)


def _topk_kernel(st_ref, idx_ref, gate_ref):
    def head(h, carry):
        v1, i1 = _topk_rows(st_ref[2 * h], PEER_TOPK, PEER_NKEYS)
        v2, i2 = _topk_rows(st_ref[2 * h + 1], PEER_TOPK, PEER_NKEYS)
        half = PEER_TOPK // 2
        sel = [(slice(0, 1), slice(None))] + [(slice(a, a + 1), slice(0, half)) for a in range(1, half)]
        sel.append((slice(half, None), slice(0, 1)))
        cs = jnp.concatenate([v1[sa] + v2[sb] for sa, sb in sel], axis=0)
        ci = jnp.concatenate([i1[sa] * PEER_NKEYS + i2[sb] for sa, sb in sel], axis=0)
        n = PEER_TOPK * PEER_TOPK
        r = lax.broadcasted_iota(I32, cs.shape, 0)
        mid = PEER_TOPK + (r - PEER_TOPK) // half * PEER_TOPK + (r - PEER_TOPK) % half
        tail = (r - (PEER_TOPK + (half - 1) * half) + half) * PEER_TOPK
        pos = jnp.where(r < PEER_TOPK, r, jnp.where(r < PEER_TOPK + (half - 1) * half, mid, tail))
        vals, ids = [], []
        for _ in range(PEER_TOPK):
            m = jnp.max(cs, axis=0, keepdims=True)
            ap = jnp.min(jnp.where(cs == m, pos, n), axis=0, keepdims=True)
            sel = pos == ap
            vals.append(m)
            ids.append(jnp.sum(jnp.where(sel, ci, 0), axis=0, keepdims=True))
            cs = jnp.where(sel, -jnp.inf, cs)
        best = jnp.concatenate(vals, axis=0)
        e = jnp.exp(best - best[0:1])
        gate_ref[h] = e / jnp.sum(e, axis=0, keepdims=True)
        idx_ref[h] = jnp.concatenate(ids, axis=0)
        return carry

    lax.fori_loop(0, PEER_HEADS, head, 0)


def _topk(st, tt):
    t = st.shape[2]
    return pl.pallas_call(
        _topk_kernel,
        grid=(t // tt,),
        in_specs=[pl.BlockSpec((2 * PEER_HEADS, PEER_NKEYS, tt), lambda i: (0, 0, i))],
        out_specs=[
            pl.BlockSpec((PEER_HEADS, PEER_TOPK, tt), lambda i: (0, 0, i)),
            pl.BlockSpec((PEER_HEADS, PEER_TOPK, tt), lambda i: (0, 0, i)),
        ],
        out_shape=[
            jax.ShapeDtypeStruct((PEER_HEADS, PEER_TOPK, t), I32),
            jax.ShapeDtypeStruct((PEER_HEADS, PEER_TOPK, t), F32),
        ],
        compiler_params=pltpu.CompilerParams(vmem_limit_bytes=VMEM_LIMIT),
        name="peer_topk",
    )(st)


def _pack_rows(a):
    n = a.shape[0]
    x = a.astype(BF16).reshape(n, ROW_WORDS, 2, LANES).transpose(0, 1, 3, 2)
    return lax.bitcast_convert_type(x, I32).reshape(n * ROW_WORDS, LANES)


def _lane_group_consts():
    lane = np.arange(PEER_PAIRS * ROW_BF16)
    diag = (lane[None, :] % ROW_BF16) == np.arange(ROW_BF16)[:, None]
    group = (lane[:, None] // ROW_BF16) == np.arange(PEER_PAIRS)[None, :]
    return jnp.asarray(diag, F32), jnp.asarray(group, BF16), jnp.asarray(group.T, BF16)


def _idx_copy(idx_hbm, idx_smem, sem, step, slot, n):
    return pltpu.make_async_copy(idx_hbm.at[pl.ds(step * n, n)], idx_smem.at[pl.ds(slot * n, n)], sem.at[slot])


def _load_table_and_indices(tab_hbm, idx_hbm, tab_v, idx_s, tab_sem, idx_sem, n):
    i = pl.program_id(0)
    slot = i % 2

    @pl.when(i == 0)
    def _():
        table = pltpu.make_async_copy(tab_hbm, tab_v, tab_sem)
        table.start()
        _idx_copy(idx_hbm, idx_s, idx_sem, 0, 0, n).start()
        table.wait()

    _idx_copy(idx_hbm, idx_s, idx_sem, i, slot, n).wait()

    @pl.when(i + 1 < pl.num_programs(0))
    def _():
        _idx_copy(idx_hbm, idx_s, idx_sem, i + 1, 1 - slot, n).start()

    return slot


GROUP = 8
TOKEN_WORDS = PEER_PAIRS * ROW_WORDS


def _gather_group(tab_v, idx_s, base, stage):
    for j in range(GROUP * PEER_PAIRS):
        e = pl.multiple_of(idx_s[base + j], ROW_WORDS)
        stage[j * ROW_WORDS:(j + 1) * ROW_WORDS, :] = tab_v[pl.ds(e, ROW_WORDS), :]


def _token_rows(stage, j):
    return pltpu.bitcast(stage[j * TOKEN_WORDS:(j + 1) * TOKEN_WORDS, :], BF16)


def _dot_hi_lo(x, w):
    hi = x.astype(BF16).astype(F32)
    r = _dot(jnp.concatenate([hi, x - hi], axis=0).astype(BF16), w)
    return r[:GROUP] + r[GROUP:]


def _pipelined_groups(tab_v, idx_s, slot, tt, stage_a, stage_b, compute):
    ng = tt // GROUP
    base = lambda g: (slot * tt + g * GROUP) * PEER_PAIRS
    _gather_group(tab_v, idx_s, base(0), stage_a)

    def pair(k, carry):
        g = 2 * k
        _gather_group(tab_v, idx_s, base(g + 1), stage_b)
        compute(stage_a, g)
        _gather_group(tab_v, idx_s, base(g + 2), stage_a)
        compute(stage_b, g + 1)
        return carry

    lax.fori_loop(0, ng // 2 - 1, pair, 0)
    _gather_group(tab_v, idx_s, base(ng - 1), stage_b)
    compute(stage_a, ng - 2)
    compute(stage_b, ng - 1)


def _peer_u_kernel(idx_hbm, tab_hbm, hp_ref, gate_ref, diag_ref, group_ref, coef_ref,
                   tab_v, idx_s, stage_a, stage_b, zbuf, act, tab_sem, idx_sem, *, tt):
    slot = _load_table_and_indices(tab_hbm, idx_hbm, tab_v, idx_s, tab_sem, idx_sem, tt * PEER_PAIRS)

    def compute(stage, g):
        for j in range(GROUP):
            r0 = pl.multiple_of((g * GROUP + j) * ROW_WORDS, ROW_WORDS)
            ht = pltpu.bitcast(hp_ref[pl.ds(r0, ROW_WORDS), :], BF16)
            z = _dot_nt(ht, _token_rows(stage, j)) * diag_ref[...]
            zbuf[j:j + 1, :] = jnp.sum(z, axis=0, keepdims=True)
        rows = pl.ds(pl.multiple_of(g * GROUP, GROUP), GROUP)
        act[rows, :] = _dot_hi_lo(zbuf[...], group_ref[...])

    _pipelined_groups(tab_v, idx_s, slot, tt, stage_a, stage_b, compute)
    a = act[...]
    gelu = 0.5 * a * (1.0 + lax.erf(a * (2.0 ** -0.5)))
    coef_ref[...] = gelu * gate_ref[...]


def _peer_v_kernel(idx_hbm, tab_hbm, coef_ref, h1_ref, diag_ref, group_t_ref, y_ref,
                   tab_v, idx_s, stage_a, stage_b, tab_sem, idx_sem, *, tt):
    slot = _load_table_and_indices(tab_hbm, idx_hbm, tab_v, idx_s, tab_sem, idx_sem, tt * PEER_PAIRS)

    def compute(stage, g):
        rows8 = pl.ds(pl.multiple_of(g * GROUP, GROUP), GROUP)
        crep = _dot_hi_lo(coef_ref[rows8, :], group_t_ref[...])
        for j in range(GROUP):
            c = jnp.broadcast_to(crep[j:j + 1, :], (ROW_BF16, crep.shape[1])) * diag_ref[...]
            out = _dot_hi_lo(c, _token_rows(stage, j))
            rows = pl.ds(pl.multiple_of((g * GROUP + j) * ROW_BF16, ROW_BF16), ROW_BF16)
            y_ref[rows, :] = h1_ref[rows, :] + out

    _pipelined_groups(tab_v, idx_s, slot, tt, stage_a, stage_b, compute)


def _peer_scratch(tt, u_phase):
    shapes = [
        pltpu.VMEM((PEER_EXPERTS * ROW_WORDS, LANES), I32),
        pltpu.SMEM((2 * tt * PEER_PAIRS,), I32),
        pltpu.VMEM((GROUP * TOKEN_WORDS, LANES), I32),
        pltpu.VMEM((GROUP * TOKEN_WORDS, LANES), I32),
    ]
    if u_phase:
        shapes += [pltpu.VMEM((GROUP, PEER_PAIRS * ROW_BF16), F32), pltpu.VMEM((tt, PEER_PAIRS), F32)]
    return shapes + [pltpu.SemaphoreType.DMA, pltpu.SemaphoreType.DMA((2,))]


def _peer_u(idx_flat, tab, hn_packed, gate, tt):
    t = gate.shape[0]
    diag, group, _ = _lane_group_consts()
    any_spec = pl.BlockSpec(memory_space=pl.ANY)
    return pl.pallas_call(
        functools.partial(_peer_u_kernel, tt=tt),
        grid=(t // tt,),
        in_specs=[
            any_spec, any_spec,
            pl.BlockSpec((tt * ROW_WORDS, LANES), lambda i: (i, 0)),
            pl.BlockSpec((tt, PEER_PAIRS), lambda i: (i, 0)),
            pl.BlockSpec(diag.shape, lambda i: (0, 0)),
            pl.BlockSpec(group.shape, lambda i: (0, 0)),
        ],
        out_specs=pl.BlockSpec((tt, PEER_PAIRS), lambda i: (i, 0)),
        out_shape=jax.ShapeDtypeStruct((t, PEER_PAIRS), F32),
        scratch_shapes=_peer_scratch(tt, True),
        compiler_params=pltpu.CompilerParams(vmem_limit_bytes=VMEM_LIMIT),
        name="peer_u",
    )(idx_flat, tab, hn_packed, gate, diag, group)


def _peer_v(idx_flat, tab, coef, h1_rows, tt):
    t = coef.shape[0]
    diag, _, group_t = _lane_group_consts()
    any_spec = pl.BlockSpec(memory_space=pl.ANY)
    return pl.pallas_call(
        functools.partial(_peer_v_kernel, tt=tt),
        grid=(t // tt,),
        in_specs=[
            any_spec, any_spec,
            pl.BlockSpec((tt, PEER_PAIRS), lambda i: (i, 0)),
            pl.BlockSpec((tt * ROW_BF16, LANES), lambda i: (i, 0)),
            pl.BlockSpec(diag.shape, lambda i: (0, 0)),
            pl.BlockSpec(group_t.shape, lambda i: (0, 0)),
        ],
        out_specs=pl.BlockSpec((tt * ROW_BF16, LANES), lambda i: (i, 0)),
        out_shape=jax.ShapeDtypeStruct((t * ROW_BF16, LANES), F32),
        scratch_shapes=_peer_scratch(tt, False),
        compiler_params=pltpu.CompilerParams(vmem_limit_bytes=VMEM_LIMIT),
        name="peer_v",
    )(idx_flat, tab, coef, h1_rows, diag, group_t)


def _rope_tables(pos):
    inv_freq = ROPE_THETA ** (-jnp.arange(ROPE_HALF, dtype=F32) / ROPE_HALF)
    ang = pos.astype(F32)[:, None] * inv_freq[None, :]
    cos, sin = jnp.cos(ang), jnp.sin(ang)
    cos_t = jnp.tile(cos, (1, LANES // ROPE_HALF))
    sin_t = jnp.tile(jnp.concatenate([-sin, sin], axis=1), (1, LANES // MLA_ROPE))
    return cos_t, sin_t


def _pack_in_proj(w_in):
    offs = np.cumsum([0, MLA_Q_RANK, MLA_KV_RANK, MLA_ROPE, GLA_DK, GLA_DK, GLA_DV, GLA_GATE_RANK, GLA_DV,
                      D_MODEL, D_MODEL])
    part = lambda i: w_in[:, int(offs[i]):int(offs[i + 1])]
    zeros = lambda n: jnp.zeros((D_MODEL, n), w_in.dtype)
    cols = [part(0), part(1), part(2), part(6), zeros(LANES - MLA_ROPE - GLA_GATE_RANK), zeros(COL_GQ - COL_KPE - LANES),
            part(3), part(4), part(5), part(7), part(8), part(9)]
    w = jnp.concatenate(cols, axis=1).astype(BF16)
    assert w.shape[1] == PROJ_COLS
    return w


def _head_major(w, widths):
    per = sum(widths)
    w = w.reshape(w.shape[0], -1, per)
    parts, o = [], 0
    for wd in widths:
        parts.append(w[:, :, o:o + wd].reshape(w.shape[0], -1))
        o += wd
    return jnp.concatenate(parts, axis=1)


def _row_tile(m, want):
    t = min(m, want)
    assert m % t == 0
    return t


def kernel(x, meta, norm_mix, w_in, mla_q_norm, mla_w_uq, mla_kv_norm, mla_w_ukv, qn_nope, qn_pe, kn_nope, kn_pe, gla_w_a2, gla_b_a, gla_norm, w_o_mla, w_o_gla, w_out, norm_ffn, peer_w_q, peer_keys, peer_u, peer_v):
    nb, r, d = x.shape
    assert d == D_MODEL and norm_mix.shape[0] == 1 and meta.shape == (N_META, D_MODEL)
    assert r % 256 == 0
    t = nb * r
    x2d = x.reshape(t, d)
    row = lambda a: a.reshape(1, -1).astype(F32)

    w1 = _pack_in_proj(w_in[0])
    g_mix = row(norm_mix[0])
    tile2 = lambda g: jnp.tile(row(g), (1, LANES // MLA_ROPE))
    gkp = jnp.concatenate([row(kn_pe[0]), jnp.ones((1, LANES - MLA_ROPE), F32)], axis=1)
    wa2 = jnp.zeros((LANES, GLA_DK), F32).at[MLA_ROPE:MLA_ROPE + GLA_GATE_RANK].set(gla_w_a2[0]).astype(BF16)
    prep_w = (
        row(mla_q_norm[0]), _head_major(mla_w_uq[0], (MLA_NOPE, MLA_ROPE)).astype(BF16),
        row(mla_kv_norm[0]), _head_major(mla_w_ukv[0], (MLA_NOPE, MLA_V)).astype(BF16),
        row(qn_nope[0]), tile2(qn_pe[0]), row(kn_nope[0]), gkp, wa2, row(gla_b_a[0]),
    )
    g_gla = row(gla_norm[0])

    proj_m = _inproj(meta.astype(F32), g_mix, w1, N_META, PROJ_COLS // 2)
    cos_m, sin_m = _rope_tables(jnp.arange(N_META))
    _, k_m, v_m, la_m = _prep(proj_m, cos_m, sin_m, prep_w, N_META, N_META)
    s_zero = jnp.zeros((GLA_HEADS, GLA_HK, GLA_HV), F32)
    _, s_meta = _gla(proj_m, la_m, s_zero, g_gla, 1, N_META)
    pad = ((0, LANES - N_META), (0, 0))
    k_m, v_m = jnp.pad(k_m, pad), jnp.pad(v_m, pad)

    proj = _inproj(x2d, g_mix, w1, _row_tile(t, 512), PROJ_COLS // 2)
    cos_t, sin_t = _rope_tables(N_META + jnp.arange(r))
    q, k, v, log_a = _prep(proj, cos_t, sin_t, prep_w, 256, r)
    y_a = _attention(q.reshape(nb, r, -1), k.reshape(nb, r, -1), v.reshape(nb, r, -1), k_m, v_m, 256)
    y_b, _ = _gla(proj, log_a, s_meta[0], g_gla, nb, CHUNK)

    h1, hn, st = _mix(y_a.reshape(t, -1), y_b, proj, x2d, w_o_mla[0].astype(BF16), w_o_gla[0].astype(BF16),
                      w_out[0].astype(BF16), row(norm_ffn[0]), peer_w_q[0].astype(BF16),
                      peer_keys[0].astype(BF16), _row_tile(t, 512))
    idx, gate = _topk(st, 256)
    idx = (idx.reshape(PEER_PAIRS, t).T * ROW_WORDS).reshape(-1)
    gate = gate.reshape(PEER_PAIRS, t).T

    tt = 128
    coef = _peer_u(idx, _pack_rows(peer_u[0]), _pack_rows(hn), gate, tt)
    y = _peer_v(idx, _pack_rows(peer_v[0]), coef, h1.reshape(t * ROW_BF16, LANES), tt)
    return y.reshape(nb, r, d)
```

```python
import functools
import math

import jax
import jax.numpy as jnp
import numpy as np
from jax import lax
from jax.experimental import pallas as pl
from jax.experimental.pallas import tpu as pltpu

F32 = jnp.float32
BF16 = jnp.bfloat16
I32 = jnp.int32

D_MODEL = 1024
CHUNK = 64
N_META = 16
EPS = 1e-6

MLA_HEADS = 8
MLA_Q_RANK = 384
MLA_KV_RANK = 256
MLA_NOPE = 128
MLA_ROPE = 64
MLA_V = 128
ROPE_HALF = MLA_ROPE // 2
ROPE_THETA = 10000.0
MLA_QK_PAD = 256
MLA_V_EXT = 256

GLA_HEADS = 4
GLA_DK = D_MODEL // 2
GLA_DV = D_MODEL
GLA_HK = GLA_DK // GLA_HEADS
GLA_HV = GLA_DV // GLA_HEADS
GLA_GATE_RANK = 16
GLA_TAU = 16.0

PEER_HEADS = 8
PEER_NKEYS = 128
PEER_EXPERTS = PEER_NKEYS * PEER_NKEYS
PEER_HALF = 128
PEER_TOPK = 16
PEER_PAIRS = PEER_HEADS * PEER_TOPK

LANES = 128
ROW_WORDS = 4
ROW_BF16 = 8
VMEM_LIMIT = 56 * 1024 * 1024

COL_CQ = 0
COL_CKV = 384
COL_KPE = 640
COL_GQ = 1024
COL_GK = 1536
COL_GV = 2048
COL_GG = 3072
COL_GATE_A = 4096
COL_GATE_B = 5120
PROJ_COLS = 6144


def _dot(a, b):
    return jnp.dot(a, b, preferred_element_type=F32)


def _dot_nt(a, b):
    return lax.dot_general(a, b, (((1,), (1,)), ((), ())), preferred_element_type=F32)


def _dot_tn(a, b):
    return lax.dot_general(a, b, (((0,), (0,)), ((), ())), preferred_element_type=F32)


def _rms(x, g):
    ms = jnp.mean(x * x, axis=-1, keepdims=True)
    return x * lax.rsqrt(ms + EPS) * g


def _split_bf16(x):
    hi = x.astype(BF16)
    lo = (x - hi.astype(F32)).astype(BF16)
    return hi, lo


def _inproj_kernel(x_ref, g_ref, w_ref, o_ref):
    n = _rms(x_ref[...], g_ref[...]).astype(BF16)
    o_ref[...] = _dot(n, w_ref[...]).astype(o_ref.dtype)


def _inproj(x2d, g, w, tm, tn):
    m, d = x2d.shape
    n = w.shape[1]
    return pl.pallas_call(
        _inproj_kernel,
        grid=(n // tn, m // tm),
        in_specs=[
            pl.BlockSpec((tm, d), lambda j, i: (i, 0)),
            pl.BlockSpec((1, d), lambda j, i: (0, 0)),
            pl.BlockSpec((d, tn), lambda j, i: (0, j)),
        ],
        out_specs=pl.BlockSpec((tm, tn), lambda j, i: (i, j)),
        out_shape=jax.ShapeDtypeStruct((m, n), BF16),
        compiler_params=pltpu.CompilerParams(vmem_limit_bytes=VMEM_LIMIT),
        name="inproj",
    )(x2d, g, w)


def _rope_pair_tile(xp, g_tile, cos, sin, lane):
    sq = xp * xp
    lo = lane < MLA_ROPE
    s_lo = jnp.sum(jnp.where(lo, sq, 0.0), axis=-1, keepdims=True)
    s_hi = jnp.sum(jnp.where(lo, 0.0, sq), axis=-1, keepdims=True)
    ms = jnp.where(lo, s_lo, s_hi) * (1.0 / MLA_ROPE)
    y = xp * lax.rsqrt(ms + EPS) * g_tile
    up = pltpu.roll(y, LANES - ROPE_HALF, axis=1)
    dn = pltpu.roll(y, ROPE_HALF, axis=1)
    partner = jnp.where((lane % MLA_ROPE) < ROPE_HALF, up, dn)
    return y * cos + partner * sin


def _prep_kernel(p_ref, cos_ref, sin_ref, qn_ref, wuq_ref, kvn_ref, wukv_ref, gqn_ref, gqp_ref,
                 gkn_ref, gkp_ref, wa2_ref, ba_ref, q_ref, k_ref, v_ref, la_ref):
    tm = p_ref.shape[0]
    lane = lax.broadcasted_iota(I32, (tm, LANES), 1)
    cos = cos_ref[...]
    sin = sin_ref[...]
    scale = (MLA_NOPE + MLA_ROPE) ** -0.5

    cq = p_ref[:, COL_CQ:COL_CQ + MLA_Q_RANK].astype(F32)
    qa = _dot(_rms(cq, qn_ref[...]).astype(BF16), wuq_ref[...])
    ckv = p_ref[:, COL_CKV:COL_CKV + MLA_KV_RANK].astype(F32)
    kva = _dot(_rms(ckv, kvn_ref[...]).astype(BF16), wukv_ref[...])

    misc = p_ref[:, COL_KPE:COL_KPE + LANES]
    kpe = _rope_pair_tile(misc.astype(F32), gkp_ref[...], cos, sin, lane)
    kpe = jnp.where(lane < MLA_ROPE, kpe, 0.0).astype(BF16)

    pe_off = MLA_HEADS * MLA_NOPE
    for j in range(MLA_HEADS // 2):
        t = _rope_pair_tile(qa[:, pe_off + j * LANES: pe_off + (j + 1) * LANES], gqp_ref[...], cos, sin, lane)
        t = t * scale
        even = jnp.where(lane < MLA_ROPE, t, 0.0)
        odd = jnp.where(lane < MLA_ROPE, pltpu.roll(t, MLA_ROPE, axis=1), 0.0)
        for h, pe in ((2 * j, even), (2 * j + 1, odd)):
            qn = _rms(qa[:, h * MLA_NOPE:(h + 1) * MLA_NOPE], gqn_ref[...]) * scale
            q_ref[:, h * MLA_QK_PAD: h * MLA_QK_PAD + MLA_NOPE] = qn.astype(BF16)
            q_ref[:, h * MLA_QK_PAD + MLA_NOPE:(h + 1) * MLA_QK_PAD] = pe.astype(BF16)
    for h in range(MLA_HEADS):
        kn = _rms(kva[:, h * MLA_NOPE:(h + 1) * MLA_NOPE], gkn_ref[...])
        k_ref[:, h * MLA_QK_PAD: h * MLA_QK_PAD + MLA_NOPE] = kn.astype(BF16)
        k_ref[:, h * MLA_QK_PAD + MLA_NOPE:(h + 1) * MLA_QK_PAD] = kpe
    v_off = MLA_HEADS * MLA_NOPE
    for h in range(MLA_HEADS):
        v_ref[:, h * MLA_V_EXT: h * MLA_V_EXT + MLA_V] = kva[:, v_off + h * MLA_V: v_off + (h + 1) * MLA_V].astype(BF16)
        v_ref[:, h * MLA_V_EXT + MLA_V:(h + 1) * MLA_V_EXT] = jnp.ones((tm, MLA_V_EXT - MLA_V), BF16)

    z = _dot(misc, wa2_ref[...]) + ba_ref[...]
    log_sig = jnp.minimum(z, 0.0) - jnp.log(1.0 + jnp.exp(-jnp.abs(z)))
    la_ref[...] = log_sig * (1.0 / GLA_TAU)


def _prep(proj, cos_t, sin_t, wts, tm, rows_per_seq):
    m = proj.shape[0]
    nblk = rows_per_seq // tm
    full = lambda a: pl.BlockSpec(a.shape, lambda i: (0,) * a.ndim)
    return pl.pallas_call(
        _prep_kernel,
        grid=(m // tm,),
        in_specs=[
            pl.BlockSpec((tm, 1024), lambda i: (i, 0)),
            pl.BlockSpec((tm, LANES), lambda i: (i % nblk, 0)),
            pl.BlockSpec((tm, LANES), lambda i: (i % nblk, 0)),
        ] + [full(w) for w in wts],
        out_specs=[
            pl.BlockSpec((tm, MLA_HEADS * MLA_QK_PAD), lambda i: (i, 0)),
            pl.BlockSpec((tm, MLA_HEADS * MLA_QK_PAD), lambda i: (i, 0)),
            pl.BlockSpec((tm, MLA_HEADS * MLA_V_EXT), lambda i: (i, 0)),
            pl.BlockSpec((tm, GLA_DK), lambda i: (i, 0)),
        ],
        out_shape=[
            jax.ShapeDtypeStruct((m, MLA_HEADS * MLA_QK_PAD), BF16),
            jax.ShapeDtypeStruct((m, MLA_HEADS * MLA_QK_PAD), BF16),
            jax.ShapeDtypeStruct((m, MLA_HEADS * MLA_V_EXT), BF16),
            jax.ShapeDtypeStruct((m, GLA_DK), F32),
        ],
        compiler_params=pltpu.CompilerParams(vmem_limit_bytes=VMEM_LIMIT),
        name="mla_gla_prep",
    )(proj, cos_t, sin_t, *wts)


def _attn_kernel(q_ref, k_ref, v_ref, km_ref, vm_ref, o_ref, *, tq, tk):
    i = pl.program_id(1)
    qs = [slice(h * MLA_QK_PAD, (h + 1) * MLA_QK_PAD) for h in range(MLA_HEADS)]
    vs = [slice(h * MLA_V_EXT, (h + 1) * MLA_V_EXT) for h in range(MLA_HEADS)]

    def update(h, state, kblk, vblk, mask):
        m, acc = state
        s = _dot_nt(q_ref[0, :, qs[h]], kblk)
        if mask is not None:
            s = jnp.where(mask, s, -jnp.inf)
        m_new = jnp.maximum(m, jnp.max(s, axis=-1, keepdims=True))
        p = jnp.exp(s - m_new).astype(BF16)
        return m_new, jnp.exp(m - m_new) * acc + _dot(p, vblk)

    meta_mask = lax.broadcasted_iota(I32, (tq, LANES), 1) < N_META
    init = (jnp.full((tq, 1), -jnp.inf, F32), jnp.zeros((tq, MLA_V_EXT), F32))
    states = tuple(update(h, init, km_ref[:, qs[h]], vm_ref[:, vs[h]], meta_mask) for h in range(MLA_HEADS))

    def body(j, states):
        r0 = pl.multiple_of(j * tk, tk)
        return tuple(update(h, states[h], k_ref[0, pl.ds(r0, tk), qs[h]], v_ref[0, pl.ds(r0, tk), vs[h]], None)
                     for h in range(MLA_HEADS))

    nfull = (i * tq) // tk
    states = lax.fori_loop(0, nfull, body, states)
    r0 = pl.multiple_of(nfull * tk, tk)
    q_chunk = (i * tq + lax.broadcasted_iota(I32, (tq, tk), 0)) // CHUNK
    k_chunk = (r0 + lax.broadcasted_iota(I32, (tq, tk), 1)) // CHUNK
    last_mask = k_chunk <= q_chunk
    for h in range(MLA_HEADS):
        _, acc = update(h, states[h], k_ref[0, pl.ds(r0, tk), qs[h]], v_ref[0, pl.ds(r0, tk), vs[h]], last_mask)
        o_ref[0, :, h * MLA_V:(h + 1) * MLA_V] = (acc[:, :MLA_V] / acc[:, MLA_V:]).astype(o_ref.dtype)


def _attention(q, k, v, k_meta, v_meta, tq, tk):
    b, r, _ = q.shape
    assert tk % tq == 0 and r % tk == 0
    return pl.pallas_call(
        functools.partial(_attn_kernel, tq=tq, tk=tk),
        grid=(b, r // tq),
        in_specs=[
            pl.BlockSpec((1, tq, q.shape[2]), lambda bi, i: (bi, i, 0)),
            pl.BlockSpec((1, r, k.shape[2]), lambda bi, i: (bi, 0, 0)),
            pl.BlockSpec((1, r, v.shape[2]), lambda bi, i: (bi, 0, 0)),
            pl.BlockSpec(k_meta.shape, lambda bi, i: (0, 0)),
            pl.BlockSpec(v_meta.shape, lambda bi, i: (0, 0)),
        ],
        out_specs=pl.BlockSpec((1, tq, MLA_HEADS * MLA_V), lambda bi, i: (bi, i, 0)),
        out_shape=jax.ShapeDtypeStruct((b, r, MLA_HEADS * MLA_V), BF16),
        compiler_params=pltpu.CompilerParams(vmem_limit_bytes=VMEM_LIMIT),
        name="mla_attention",
    )(q, k, v, k_meta, v_meta)


def _gla_level_consts(c):
    idx = np.arange(c)
    tril = (idx[None, :] <= idx[:, None]).astype(np.float32)
    sels, masks = [], []
    m = c // 2
    while m >= 1:
        ref = (idx // (2 * m)) * (2 * m) + m - 1
        sels.append(tril[ref])
        same = (idx[:, None] // (2 * m)) == (idx[None, :] // (2 * m))
        masks.append(same & ((idx[:, None] % (2 * m)) >= m) & ((idx[None, :] % (2 * m)) < m))
        m //= 2
    return tril, np.stack(sels), np.stack(masks).astype(np.float32)


def _gla_kernel(gq_ref, gk_ref, gv_ref, gg_ref, la_ref, s0_ref, gn_ref, tril_ref, sel_ref, msk_ref,
                y_ref, sout_ref, s_scr, *, c, nlev):
    ci = pl.program_id(1)

    @pl.when(ci == 0)
    def _():
        s_scr[...] = s0_ref[...]

    hp = lax.Precision.HIGHEST
    la = la_ref[...]
    b_all = jnp.dot(tril_ref[...], la, precision=hp, preferred_element_type=F32)
    brefs = [jnp.dot(sel_ref[lv], la, precision=hp, preferred_element_type=F32) for lv in range(nlev)]
    eye = (lax.broadcasted_iota(I32, (c, c), 0) == lax.broadcasted_iota(I32, (c, c), 1))
    eye_k = (lax.broadcasted_iota(I32, (GLA_HK, GLA_HK), 0) == lax.broadcasted_iota(I32, (GLA_HK, GLA_HK), 1))

    for h in range(GLA_HEADS):
        ks = slice(h * GLA_HK, (h + 1) * GLA_HK)
        vs = slice(h * GLA_HV, (h + 1) * GLA_HV)
        q = gq_ref[:, ks].astype(F32) * (GLA_HK ** -0.5)
        k = gk_ref[:, ks].astype(F32)
        v = gv_ref[:, vs]
        bh = b_all[:, ks]
        s_old = s_scr[h]

        o = _dot((q * jnp.exp(bh)).astype(BF16), s_old.astype(BF16))
        a = jnp.where(eye, _dot_nt(q.astype(BF16), k.astype(BF16)), 0.0)
        for lv in range(nlev):
            br = brefs[lv][:, ks]
            qd = q * jnp.exp(jnp.minimum(bh - br, 0.0))
            kd = k * jnp.exp(jnp.minimum(br - bh, 0.0))
            a = a + msk_ref[lv] * _dot_nt(qd.astype(BF16), kd.astype(BF16))
        o = o + _dot(a.astype(BF16), v)

        b_last = bh[c - 1:c, :]
        kdec = k * jnp.exp(b_last - bh)
        dec_col = jnp.sum(jnp.where(eye_k, jnp.exp(b_last), 0.0), axis=1, keepdims=True)
        s_scr[h] = dec_col * s_old + _dot_tn(kdec.astype(BF16), v)

        gg = gg_ref[:, vs].astype(F32)
        silu = gg / (1.0 + jnp.exp(-gg))
        y_ref[:, vs] = (_rms(o, gn_ref[...]) * silu).astype(y_ref.dtype)

    @pl.when(ci == pl.num_programs(1) - 1)
    def _():
        sout_ref[0] = s_scr[...]


def _gla(proj, log_a, s0, g_norm, nb, c):
    m = proj.shape[0]
    nc = m // nb // c
    tril, sels, masks = _gla_level_consts(c)
    nlev = sels.shape[0]
    row = lambda bi, ci: bi * nc + ci
    return pl.pallas_call(
        functools.partial(_gla_kernel, c=c, nlev=nlev),
        grid=(nb, nc),
        in_specs=[
            pl.BlockSpec((c, GLA_DK), lambda bi, ci: (row(bi, ci), COL_GQ // GLA_DK)),
            pl.BlockSpec((c, GLA_DK), lambda bi, ci: (row(bi, ci), COL_GK // GLA_DK)),
            pl.BlockSpec((c, GLA_DV), lambda bi, ci: (row(bi, ci), COL_GV // GLA_DV)),
            pl.BlockSpec((c, GLA_DV), lambda bi, ci: (row(bi, ci), COL_GG // GLA_DV)),
            pl.BlockSpec((c, GLA_DK), lambda bi, ci: (row(bi, ci), 0)),
            pl.BlockSpec(s0.shape, lambda bi, ci: (0, 0, 0)),
            pl.BlockSpec(g_norm.shape, lambda bi, ci: (0, 0)),
            pl.BlockSpec(tril.shape, lambda bi, ci: (0, 0)),
            pl.BlockSpec(sels.shape, lambda bi, ci: (0, 0, 0)),
            pl.BlockSpec(masks.shape, lambda bi, ci: (0, 0, 0)),
        ],
        out_specs=[
            pl.BlockSpec((c, GLA_DV), lambda bi, ci: (row(bi, ci), 0)),
            pl.BlockSpec((1,) + s0.shape, lambda bi, ci: (bi, 0, 0, 0)),
        ],
        out_shape=[
            jax.ShapeDtypeStruct((m, GLA_DV), BF16),
            jax.ShapeDtypeStruct((nb,) + s0.shape, F32),
        ],
        scratch_shapes=[pltpu.VMEM(s0.shape, F32)],
        compiler_params=pltpu.CompilerParams(vmem_limit_bytes=VMEM_LIMIT),
        name="gla",
    )(proj, proj, proj, proj, log_a, s0, g_norm, jnp.asarray(tril), jnp.asarray(sels), jnp.asarray(masks))


def _mix_kernel(ya_ref, yb_ref, ga_ref, gb_ref, x_ref, wa_ref, wb_ref, wo_ref, gf_ref, wq_ref, keys_ref,
                h1_ref, hn_ref, st_ref):
    sig = lambda t: 1.0 / (1.0 + jnp.exp(-t))
    a = _dot(ya_ref[...], wa_ref[...])
    b = _dot(yb_ref[...], wb_ref[...])
    mix = sig(ga_ref[...].astype(F32)) * a + sig(gb_ref[...].astype(F32)) * b
    h1 = x_ref[...] + _dot(mix.astype(BF16), wo_ref[...])
    h1_ref[...] = h1
    hn = _rms(h1, gf_ref[...]).astype(BF16)
    hn_ref[...] = hn
    qp = _dot(hn, wq_ref[...]).astype(BF16)
    for hp in range(2 * PEER_HEADS):
        st_ref[hp] = _dot_nt(keys_ref[hp % 2], qp[:, hp * PEER_HALF:(hp + 1) * PEER_HALF])


def _mix(y_a, y_b, proj, x2d, w_a, w_b, w_o, g_ffn, w_q, keys, tm):
    m = x2d.shape[0]
    full = lambda a: pl.BlockSpec(a.shape, lambda i: (0,) * a.ndim)
    return pl.pallas_call(
        _mix_kernel,
        grid=(m // tm,),
        in_specs=[
            pl.BlockSpec((tm, D_MODEL), lambda i: (i, 0)),
            pl.BlockSpec((tm, D_MODEL), lambda i: (i, 0)),
            pl.BlockSpec((tm, D_MODEL), lambda i: (i, COL_GATE_A // D_MODEL)),
            pl.BlockSpec((tm, D_MODEL), lambda i: (i, COL_GATE_B // D_MODEL)),
            pl.BlockSpec((tm, D_MODEL), lambda i: (i, 0)),
            full(w_a), full(w_b), full(w_o), full(g_ffn), full(w_q), full(keys),
        ],
        out_specs=[
            pl.BlockSpec((tm, D_MODEL), lambda i: (i, 0)),
            pl.BlockSpec((tm, D_MODEL), lambda i: (i, 0)),
            pl.BlockSpec((2 * PEER_HEADS, PEER_NKEYS, tm), lambda i: (0, 0, i)),
        ],
        out_shape=[
            jax.ShapeDtypeStruct((m, D_MODEL), F32),
            jax.ShapeDtypeStruct((m, D_MODEL), BF16),
            jax.ShapeDtypeStruct((2 * PEER_HEADS, PEER_NKEYS, m), F32),
        ],
        compiler_params=pltpu.CompilerParams(vmem_limit_bytes=VMEM_LIMIT),
        name="mix_out_peer_scores",
    )(y_a, y_b, proj, proj, x2d, w_a, w_b, w_o, g_ffn, w_q, keys)


def _topk_rows(s, k, n):
    rid = lax.broadcasted_iota(I32, s.shape, 0)
    vals, ids = [], []
    for _ in range(k):
        m = jnp.max(s, axis=0, keepdims=True)
        am = jnp.min(jnp.where(s == m, rid, n), axis=0, keepdims=True)
        vals.append(m)
        ids.append(am)
        s = jnp.where(rid == am, -jnp.inf, s)
    return jnp.concatenate(vals, axis=0), jnp.concatenate(ids, axis=0)


def _topk_kernel(st_ref, idx_out_ref, gate_out_ref, idx_ref, gate_ref):
    def head(h, carry):
        v1, i1 = _topk_rows(st_ref[2 * h], PEER_TOPK, PEER_NKEYS)
        v2, i2 = _topk_rows(st_ref[2 * h + 1], PEER_TOPK, PEER_NKEYS)
        half = PEER_TOPK // 2
        sel = [(slice(0, 1), slice(None))] + [(slice(a, a + 1), slice(0, half)) for a in range(1, half)]
        sel.append((slice(half, None), slice(0, 1)))
        cs = jnp.concatenate([v1[sa] + v2[sb] for sa, sb in sel], axis=0)
        ci = jnp.concatenate([i1[sa] * PEER_NKEYS + i2[sb] for sa, sb in sel], axis=0)
        n = PEER_TOPK * PEER_TOPK
        r = lax.broadcasted_iota(I32, cs.shape, 0)
        mid = PEER_TOPK + (r - PEER_TOPK) // half * PEER_TOPK + (r - PEER_TOPK) % half
        tail = (r - (PEER_TOPK + (half - 1) * half) + half) * PEER_TOPK
        pos = jnp.where(r < PEER_TOPK, r, jnp.where(r < PEER_TOPK + (half - 1) * half, mid, tail))
        vals, ids = [], []
        for _ in range(PEER_TOPK):
            m = jnp.max(cs, axis=0, keepdims=True)
            ap = jnp.min(jnp.where(cs == m, pos, n), axis=0, keepdims=True)
            sel = pos == ap
            vals.append(m)
            ids.append(jnp.sum(jnp.where(sel, ci, 0), axis=0, keepdims=True))
            cs = jnp.where(sel, -jnp.inf, cs)
        best = jnp.concatenate(vals, axis=0)
        e = jnp.exp(best - best[0:1])
        gate_ref[h] = e / jnp.sum(e, axis=0, keepdims=True)
        idx_ref[h] = jnp.concatenate(ids, axis=0)
        return carry

    lax.fori_loop(0, PEER_HEADS, head, 0)
    tt = idx_ref.shape[2]
    idx_out_ref[...] = idx_ref[...].reshape(PEER_PAIRS, tt).T * ROW_WORDS
    gate_out_ref[...] = gate_ref[...].reshape(PEER_PAIRS, tt).T


def _topk(st, tt):
    t = st.shape[2]
    return pl.pallas_call(
        _topk_kernel,
        grid=(t // tt,),
        in_specs=[pl.BlockSpec((2 * PEER_HEADS, PEER_NKEYS, tt), lambda i: (0, 0, i))],
        out_specs=[
            pl.BlockSpec((tt, PEER_PAIRS), lambda i: (i, 0)),
            pl.BlockSpec((tt, PEER_PAIRS), lambda i: (i, 0)),
        ],
        out_shape=[
            jax.ShapeDtypeStruct((t, PEER_PAIRS), I32),
            jax.ShapeDtypeStruct((t, PEER_PAIRS), F32),
        ],
        scratch_shapes=[
            pltpu.VMEM((PEER_HEADS, PEER_TOPK, tt), I32),
            pltpu.VMEM((PEER_HEADS, PEER_TOPK, tt), F32),
        ],
        compiler_params=pltpu.CompilerParams(vmem_limit_bytes=VMEM_LIMIT),
        name="peer_topk",
    )(st)


def _pack_kernel(x_ref, o_ref):
    rows = x_ref.shape[0]
    x = x_ref[...].astype(F32).reshape(rows * ROW_BF16, LANES)
    o_ref[...] = pltpu.bitcast(x.astype(BF16), I32)


def _pack_rows(a, rows):
    n = a.shape[0]
    return pl.pallas_call(
        _pack_kernel,
        grid=(n // rows,),
        in_specs=[pl.BlockSpec((rows, a.shape[1]), lambda i: (i, 0))],
        out_specs=pl.BlockSpec((rows * ROW_WORDS, LANES), lambda i: (i, 0)),
        out_shape=jax.ShapeDtypeStruct((n * ROW_WORDS, LANES), I32),
        compiler_params=pltpu.CompilerParams(vmem_limit_bytes=VMEM_LIMIT),
        name="pack_rows",
    )(a)


def _lane_group_consts():
    lane = np.arange(PEER_PAIRS * ROW_BF16)
    diag = (lane[None, :] % ROW_BF16) == np.arange(ROW_BF16)[:, None]
    group = (lane[:, None] // ROW_BF16) == np.arange(PEER_PAIRS)[None, :]
    return jnp.asarray(diag, F32), jnp.asarray(group, BF16), jnp.asarray(group.T, BF16)


def _idx_copy(idx_hbm, idx_smem, sem, step, slot, n):
    return pltpu.make_async_copy(idx_hbm.at[pl.ds(step * n, n)], idx_smem.at[pl.ds(slot * n, n)], sem.at[slot])


def _load_table_and_indices(tab_hbm, idx_hbm, tab_v, idx_s, tab_sem, idx_sem, n):
    i = pl.program_id(0)
    slot = i % 2

    @pl.when(i == 0)
    def _():
        table = pltpu.make_async_copy(tab_hbm, tab_v, tab_sem)
        table.start()
        _idx_copy(idx_hbm, idx_s, idx_sem, 0, 0, n).start()
        table.wait()

    _idx_copy(idx_hbm, idx_s, idx_sem, i, slot, n).wait()

    @pl.when(i + 1 < pl.num_programs(0))
    def _():
        _idx_copy(idx_hbm, idx_s, idx_sem, i + 1, 1 - slot, n).start()

    return slot


GROUP = 8
TOKEN_WORDS = PEER_PAIRS * ROW_WORDS


def _gather_group(tab_v, idx_s, base, stage):
    for j in range(GROUP * PEER_PAIRS):
        e = pl.multiple_of(idx_s[base + j], ROW_WORDS)
        stage[j * ROW_WORDS:(j + 1) * ROW_WORDS, :] = tab_v[pl.ds(e, ROW_WORDS), :]


def _token_rows(stage, j):
    return pltpu.bitcast(stage[j * TOKEN_WORDS:(j + 1) * TOKEN_WORDS, :], BF16)


def _dot_hi_lo(x, w):
    hi = x.astype(BF16).astype(F32)
    r = _dot(jnp.concatenate([hi, x - hi], axis=0).astype(BF16), w)
    return r[:GROUP] + r[GROUP:]


def _pipelined_groups(tab_v, idx_s, slot, tt, stage_a, stage_b, compute):
    ng = tt // GROUP
    base = lambda g: (slot * tt + g * GROUP) * PEER_PAIRS
    _gather_group(tab_v, idx_s, base(0), stage_a)

    def pair(k, carry):
        g = 2 * k
        _gather_group(tab_v, idx_s, base(g + 1), stage_b)
        compute(stage_a, g)
        _gather_group(tab_v, idx_s, base(g + 2), stage_a)
        compute(stage_b, g + 1)
        return carry

    lax.fori_loop(0, ng // 2 - 1, pair, 0)
    _gather_group(tab_v, idx_s, base(ng - 1), stage_b)
    compute(stage_a, ng - 2)
    compute(stage_b, ng - 1)


def _peer_u_kernel(idx_hbm, tab_hbm, hp_ref, gate_ref, diag_ref, group_ref, coef_ref,
                   tab_v, idx_s, stage_a, stage_b, zbuf, act, tab_sem, idx_sem, *, tt):
    slot = _load_table_and_indices(tab_hbm, idx_hbm, tab_v, idx_s, tab_sem, idx_sem, tt * PEER_PAIRS)

    def compute(stage, g):
        for j in range(GROUP):
            r0 = pl.multiple_of((g * GROUP + j) * ROW_WORDS, ROW_WORDS)
            ht = pltpu.bitcast(hp_ref[pl.ds(r0, ROW_WORDS), :], BF16)
            z = _dot_nt(ht, _token_rows(stage, j)) * diag_ref[...]
            zbuf[j:j + 1, :] = jnp.sum(z, axis=0, keepdims=True)
        rows = pl.ds(pl.multiple_of(g * GROUP, GROUP), GROUP)
        act[rows, :] = _dot_hi_lo(zbuf[...], group_ref[...])

    _pipelined_groups(tab_v, idx_s, slot, tt, stage_a, stage_b, compute)
    a = act[...]
    gelu = 0.5 * a * (1.0 + lax.erf(a * (2.0 ** -0.5)))
    coef_ref[...] = gelu * gate_ref[...]


def _peer_v_kernel(idx_hbm, tab_hbm, coef_ref, h1_ref, diag_ref, group_t_ref, y_ref,
                   tab_v, idx_s, stage_a, stage_b, tab_sem, idx_sem, *, tt):
    slot = _load_table_and_indices(tab_hbm, idx_hbm, tab_v, idx_s, tab_sem, idx_sem, tt * PEER_PAIRS)

    def compute(stage, g):
        rows8 = pl.ds(pl.multiple_of(g * GROUP, GROUP), GROUP)
        crep = _dot_hi_lo(coef_ref[rows8, :], group_t_ref[...])
        outs = []
        for j in range(GROUP):
            c = jnp.broadcast_to(crep[j:j + 1, :], (ROW_BF16, crep.shape[1])) * diag_ref[...]
            outs.append(_dot_hi_lo(c, _token_rows(stage, j)))
        y_ref[rows8, :] = h1_ref[rows8, :] + jnp.stack(outs).reshape(GROUP, D_MODEL)

    _pipelined_groups(tab_v, idx_s, slot, tt, stage_a, stage_b, compute)


def _peer_scratch(tt, u_phase):
    shapes = [
        pltpu.VMEM((PEER_EXPERTS * ROW_WORDS, LANES), I32),
        pltpu.SMEM((2 * tt * PEER_PAIRS,), I32),
        pltpu.VMEM((GROUP * TOKEN_WORDS, LANES), I32),
        pltpu.VMEM((GROUP * TOKEN_WORDS, LANES), I32),
    ]
    if u_phase:
        shapes += [pltpu.VMEM((GROUP, PEER_PAIRS * ROW_BF16), F32), pltpu.VMEM((tt, PEER_PAIRS), F32)]
    return shapes + [pltpu.SemaphoreType.DMA, pltpu.SemaphoreType.DMA((2,))]


def _peer_u(idx_flat, tab, hn_packed, gate, tt):
    t = gate.shape[0]
    diag, group, _ = _lane_group_consts()
    any_spec = pl.BlockSpec(memory_space=pl.ANY)
    return pl.pallas_call(
        functools.partial(_peer_u_kernel, tt=tt),
        grid=(t // tt,),
        in_specs=[
            any_spec, any_spec,
            pl.BlockSpec((tt * ROW_WORDS, LANES), lambda i: (i, 0)),
            pl.BlockSpec((tt, PEER_PAIRS), lambda i: (i, 0)),
            pl.BlockSpec(diag.shape, lambda i: (0, 0)),
            pl.BlockSpec(group.shape, lambda i: (0, 0)),
        ],
        out_specs=pl.BlockSpec((tt, PEER_PAIRS), lambda i: (i, 0)),
        out_shape=jax.ShapeDtypeStruct((t, PEER_PAIRS), F32),
        scratch_shapes=_peer_scratch(tt, True),
        compiler_params=pltpu.CompilerParams(vmem_limit_bytes=VMEM_LIMIT),
        name="peer_u",
    )(idx_flat, tab, hn_packed, gate, diag, group)


def _peer_v(idx_flat, tab, coef, h1_rows, tt):
    t = coef.shape[0]
    diag, _, group_t = _lane_group_consts()
    any_spec = pl.BlockSpec(memory_space=pl.ANY)
    return pl.pallas_call(
        functools.partial(_peer_v_kernel, tt=tt),
        grid=(t // tt,),
        in_specs=[
            any_spec, any_spec,
            pl.BlockSpec((tt, PEER_PAIRS), lambda i: (i, 0)),
            pl.BlockSpec((tt, D_MODEL), lambda i: (i, 0)),
            pl.BlockSpec(diag.shape, lambda i: (0, 0)),
            pl.BlockSpec(group_t.shape, lambda i: (0, 0)),
        ],
        out_specs=pl.BlockSpec((tt, D_MODEL), lambda i: (i, 0)),
        out_shape=jax.ShapeDtypeStruct((t, D_MODEL), F32),
        scratch_shapes=_peer_scratch(tt, False),
        compiler_params=pltpu.CompilerParams(vmem_limit_bytes=VMEM_LIMIT),
        name="peer_v",
    )(idx_flat, tab, coef, h1_rows, diag, group_t)


def _rope_tables(pos):
    inv_freq = ROPE_THETA ** (-jnp.arange(ROPE_HALF, dtype=F32) / ROPE_HALF)
    ang = pos.astype(F32)[:, None] * inv_freq[None, :]
    cos, sin = jnp.cos(ang), jnp.sin(ang)
    cos_t = jnp.tile(cos, (1, LANES // ROPE_HALF))
    sin_t = jnp.tile(jnp.concatenate([-sin, sin], axis=1), (1, LANES // MLA_ROPE))
    return cos_t, sin_t


def _pack_in_proj(w_in):
    offs = np.cumsum([0, MLA_Q_RANK, MLA_KV_RANK, MLA_ROPE, GLA_DK, GLA_DK, GLA_DV, GLA_GATE_RANK, GLA_DV,
                      D_MODEL, D_MODEL])
    part = lambda i: w_in[:, int(offs[i]):int(offs[i + 1])]
    zeros = lambda n: jnp.zeros((D_MODEL, n), w_in.dtype)
    cols = [part(0), part(1), part(2), part(6), zeros(LANES - MLA_ROPE - GLA_GATE_RANK), zeros(COL_GQ - COL_KPE - LANES),
            part(3), part(4), part(5), part(7), part(8), part(9)]
    w = jnp.concatenate(cols, axis=1).astype(BF16)
    assert w.shape[1] == PROJ_COLS
    return w


def _head_major(w, widths):
    per = sum(widths)
    w = w.reshape(w.shape[0], -1, per)
    parts, o = [], 0
    for wd in widths:
        parts.append(w[:, :, o:o + wd].reshape(w.shape[0], -1))
        o += wd
    return jnp.concatenate(parts, axis=1)


def _row_tile(m, want):
    t = min(m, want)
    assert m % t == 0
    return t


def kernel(x, meta, norm_mix, w_in, mla_q_norm, mla_w_uq, mla_kv_norm, mla_w_ukv, qn_nope, qn_pe, kn_nope, kn_pe, gla_w_a2, gla_b_a, gla_norm, w_o_mla, w_o_gla, w_out, norm_ffn, peer_w_q, peer_keys, peer_u, peer_v):
    nb, r, d = x.shape
    assert d == D_MODEL and norm_mix.shape[0] == 1 and meta.shape == (N_META, D_MODEL)
    assert r % 256 == 0
    t = nb * r
    x2d = x.reshape(t, d)
    row = lambda a: a.reshape(1, -1).astype(F32)

    w1 = _pack_in_proj(w_in[0])
    g_mix = row(norm_mix[0])
    tile2 = lambda g: jnp.tile(row(g), (1, LANES // MLA_ROPE))
    gkp = jnp.concatenate([row(kn_pe[0]), jnp.ones((1, LANES - MLA_ROPE), F32)], axis=1)
    wa2 = jnp.zeros((LANES, GLA_DK), F32).at[MLA_ROPE:MLA_ROPE + GLA_GATE_RANK].set(gla_w_a2[0]).astype(BF16)
    prep_w = (
        row(mla_q_norm[0]), _head_major(mla_w_uq[0], (MLA_NOPE, MLA_ROPE)).astype(BF16),
        row(mla_kv_norm[0]), _head_major(mla_w_ukv[0], (MLA_NOPE, MLA_V)).astype(BF16),
        row(qn_nope[0]), tile2(qn_pe[0]), row(kn_nope[0]), gkp, wa2, row(gla_b_a[0]),
    )
    g_gla = row(gla_norm[0])

    proj_m = _inproj(meta.astype(F32), g_mix, w1, N_META, PROJ_COLS // 2)
    cos_m, sin_m = _rope_tables(jnp.arange(N_META))
    _, k_m, v_m, la_m = _prep(proj_m, cos_m, sin_m, prep_w, N_META, N_META)
    s_zero = jnp.zeros((GLA_HEADS, GLA_HK, GLA_HV), F32)
    _, s_meta = _gla(proj_m, la_m, s_zero, g_gla, 1, N_META)
    pad = ((0, LANES - N_META), (0, 0))
    k_m, v_m = jnp.pad(k_m, pad), jnp.pad(v_m, pad)

    proj = _inproj(x2d, g_mix, w1, _row_tile(t, 512), PROJ_COLS // 2)
    cos_t, sin_t = _rope_tables(N_META + jnp.arange(r))
    q, k, v, log_a = _prep(proj, cos_t, sin_t, prep_w, 256, r)
    y_a = _attention(q.reshape(nb, r, -1), k.reshape(nb, r, -1), v.reshape(nb, r, -1), k_m, v_m, 256, 512)
    y_b, _ = _gla(proj, log_a, s_meta[0], g_gla, nb, CHUNK)

    h1, hn, st = _mix(y_a.reshape(t, -1), y_b, proj, x2d, w_o_mla[0].astype(BF16), w_o_gla[0].astype(BF16),
                      w_out[0].astype(BF16), row(norm_ffn[0]), peer_w_q[0].astype(BF16),
                      peer_keys[0].astype(BF16), _row_tile(t, 512))
    idx, gate = _topk(st, 256)
    idx = idx.reshape(-1)

    tt = 128
    coef = _peer_u(idx, _pack_rows(peer_u[0], 512), _pack_rows(hn, 512), gate, tt)
    y = _peer_v(idx, _pack_rows(peer_v[0], 512), coef, h1, tt)
    return y.reshape(nb, r, d)
```

```python
import functools
import math

import jax
import jax.numpy as jnp
import numpy as np
from jax import lax
from jax.experimental import pallas as pl
from jax.experimental.pallas import tpu as pltpu

F32 = jnp.float32
BF16 = jnp.bfloat16
I32 = jnp.int32

D_MODEL = 1024
CHUNK = 64
N_META = 16
EPS = 1e-6

MLA_HEADS = 8
MLA_Q_RANK = 384
MLA_KV_RANK = 256
MLA_NOPE = 128
MLA_ROPE = 64
MLA_V = 128
ROPE_HALF = MLA_ROPE // 2
ROPE_THETA = 10000.0
MLA_QK_PAD = 256
MLA_V_EXT = 256

GLA_HEADS = 4
GLA_DK = D_MODEL // 2
GLA_DV = D_MODEL
GLA_HK = GLA_DK // GLA_HEADS
GLA_HV = GLA_DV // GLA_HEADS
GLA_GATE_RANK = 16
GLA_TAU = 16.0

PEER_HEADS = 8
PEER_NKEYS = 128
PEER_EXPERTS = PEER_NKEYS * PEER_NKEYS
PEER_HALF = 128
PEER_TOPK = 16
PEER_PAIRS = PEER_HEADS * PEER_TOPK

LANES = 128
ROW_WORDS = 4
ROW_BF16 = 8
VMEM_LIMIT = 56 * 1024 * 1024

COL_CQ = 0
COL_CKV = 384
COL_KPE = 640
COL_GQ = 1024
COL_GK = 1536
COL_GV = 2048
COL_GG = 3072
COL_GATE_A = 4096
COL_GATE_B = 5120
PROJ_COLS = 6144


def _dot(a, b):
    return jnp.dot(a, b, preferred_element_type=F32)


def _dot_nt(a, b):
    return lax.dot_general(a, b, (((1,), (1,)), ((), ())), preferred_element_type=F32)


def _dot_tn(a, b):
    return lax.dot_general(a, b, (((0,), (0,)), ((), ())), preferred_element_type=F32)


def _rms(x, g):
    ms = jnp.mean(x * x, axis=-1, keepdims=True)
    return x * lax.rsqrt(ms + EPS) * g


def _split_bf16(x):
    hi = x.astype(BF16)
    lo = (x - hi.astype(F32)).astype(BF16)
    return hi, lo


def _inproj_kernel(x_ref, g_ref, w_ref, o_ref):
    n = _rms(x_ref[...], g_ref[...]).astype(BF16)
    o_ref[...] = _dot(n, w_ref[...]).astype(o_ref.dtype)


def _inproj(x2d, g, w, tm, tn):
    m, d = x2d.shape
    n = w.shape[1]
    return pl.pallas_call(
        _inproj_kernel,
        grid=(n // tn, m // tm),
        in_specs=[
            pl.BlockSpec((tm, d), lambda j, i: (i, 0)),
            pl.BlockSpec((1, d), lambda j, i: (0, 0)),
            pl.BlockSpec((d, tn), lambda j, i: (0, j)),
        ],
        out_specs=pl.BlockSpec((tm, tn), lambda j, i: (i, j)),
        out_shape=jax.ShapeDtypeStruct((m, n), BF16),
        compiler_params=pltpu.CompilerParams(vmem_limit_bytes=VMEM_LIMIT),
        name="inproj",
    )(x2d, g, w)


def _rope_pair_tile(xp, g_tile, cos, sin, lane):
    sq = xp * xp
    lo = lane < MLA_ROPE
    s_lo = jnp.sum(jnp.where(lo, sq, 0.0), axis=-1, keepdims=True)
    s_hi = jnp.sum(jnp.where(lo, 0.0, sq), axis=-1, keepdims=True)
    ms = jnp.where(lo, s_lo, s_hi) * (1.0 / MLA_ROPE)
    y = xp * lax.rsqrt(ms + EPS) * g_tile
    up = pltpu.roll(y, LANES - ROPE_HALF, axis=1)
    dn = pltpu.roll(y, ROPE_HALF, axis=1)
    partner = jnp.where((lane % MLA_ROPE) < ROPE_HALF, up, dn)
    return y * cos + partner * sin


def _prep_kernel(p_ref, cos_ref, sin_ref, qn_ref, wuq_ref, kvn_ref, wukv_ref, gqn_ref, gqp_ref,
                 gkn_ref, gkp_ref, wa2_ref, ba_ref, q_ref, k_ref, v_ref, la_ref):
    tm = p_ref.shape[0]
    lane = lax.broadcasted_iota(I32, (tm, LANES), 1)
    cos = cos_ref[...]
    sin = sin_ref[...]
    scale = (MLA_NOPE + MLA_ROPE) ** -0.5

    cq = p_ref[:, COL_CQ:COL_CQ + MLA_Q_RANK].astype(F32)
    qa = _dot(_rms(cq, qn_ref[...]).astype(BF16), wuq_ref[...])
    ckv = p_ref[:, COL_CKV:COL_CKV + MLA_KV_RANK].astype(F32)
    kva = _dot(_rms(ckv, kvn_ref[...]).astype(BF16), wukv_ref[...])

    misc = p_ref[:, COL_KPE:COL_KPE + LANES]
    kpe = _rope_pair_tile(misc.astype(F32), gkp_ref[...], cos, sin, lane)
    kpe = jnp.where(lane < MLA_ROPE, kpe, 0.0).astype(BF16)

    pe_off = MLA_HEADS * MLA_NOPE
    for j in range(MLA_HEADS // 2):
        t = _rope_pair_tile(qa[:, pe_off + j * LANES: pe_off + (j + 1) * LANES], gqp_ref[...], cos, sin, lane)
        t = t * scale
        even = jnp.where(lane < MLA_ROPE, t, 0.0)
        odd = jnp.where(lane < MLA_ROPE, pltpu.roll(t, MLA_ROPE, axis=1), 0.0)
        for h, pe in ((2 * j, even), (2 * j + 1, odd)):
            qn = _rms(qa[:, h * MLA_NOPE:(h + 1) * MLA_NOPE], gqn_ref[...]) * scale
            q_ref[:, h * MLA_QK_PAD: h * MLA_QK_PAD + MLA_NOPE] = qn.astype(BF16)
            q_ref[:, h * MLA_QK_PAD + MLA_NOPE:(h + 1) * MLA_QK_PAD] = pe.astype(BF16)
    for h in range(MLA_HEADS):
        kn = _rms(kva[:, h * MLA_NOPE:(h + 1) * MLA_NOPE], gkn_ref[...])
        k_ref[:, h * MLA_QK_PAD: h * MLA_QK_PAD + MLA_NOPE] = kn.astype(BF16)
        k_ref[:, h * MLA_QK_PAD + MLA_NOPE:(h + 1) * MLA_QK_PAD] = kpe
    v_off = MLA_HEADS * MLA_NOPE
    for h in range(MLA_HEADS):
        v_ref[:, h * MLA_V_EXT: h * MLA_V_EXT + MLA_V] = kva[:, v_off + h * MLA_V: v_off + (h + 1) * MLA_V].astype(BF16)
        v_ref[:, h * MLA_V_EXT + MLA_V:(h + 1) * MLA_V_EXT] = jnp.ones((tm, MLA_V_EXT - MLA_V), BF16)

    z = _dot(misc, wa2_ref[...]) + ba_ref[...]
    log_sig = jnp.minimum(z, 0.0) - jnp.log(1.0 + jnp.exp(-jnp.abs(z)))
    la_ref[...] = log_sig * (1.0 / GLA_TAU)


def _prep(proj, cos_t, sin_t, wts, tm, rows_per_seq):
    m = proj.shape[0]
    nblk = rows_per_seq // tm
    full = lambda a: pl.BlockSpec(a.shape, lambda i: (0,) * a.ndim)
    return pl.pallas_call(
        _prep_kernel,
        grid=(m // tm,),
        in_specs=[
            pl.BlockSpec((tm, 1024), lambda i: (i, 0)),
            pl.BlockSpec((tm, LANES), lambda i: (i % nblk, 0)),
            pl.BlockSpec((tm, LANES), lambda i: (i % nblk, 0)),
        ] + [full(w) for w in wts],
        out_specs=[
            pl.BlockSpec((tm, MLA_HEADS * MLA_QK_PAD), lambda i: (i, 0)),
            pl.BlockSpec((tm, MLA_HEADS * MLA_QK_PAD), lambda i: (i, 0)),
            pl.BlockSpec((tm, MLA_HEADS * MLA_V_EXT), lambda i: (i, 0)),
            pl.BlockSpec((tm, GLA_DK), lambda i: (i, 0)),
        ],
        out_shape=[
            jax.ShapeDtypeStruct((m, MLA_HEADS * MLA_QK_PAD), BF16),
            jax.ShapeDtypeStruct((m, MLA_HEADS * MLA_QK_PAD), BF16),
            jax.ShapeDtypeStruct((m, MLA_HEADS * MLA_V_EXT), BF16),
            jax.ShapeDtypeStruct((m, GLA_DK), F32),
        ],
        compiler_params=pltpu.CompilerParams(vmem_limit_bytes=VMEM_LIMIT),
        name="mla_gla_prep",
    )(proj, cos_t, sin_t, *wts)


def _attn_kernel(q_ref, k_ref, v_ref, km_ref, vm_ref, o_ref, *, tq, tk):
    i = pl.program_id(1)
    qs = [slice(h * MLA_QK_PAD, (h + 1) * MLA_QK_PAD) for h in range(MLA_HEADS)]
    vs = [slice(h * MLA_V_EXT, (h + 1) * MLA_V_EXT) for h in range(MLA_HEADS)]

    def update(h, state, kblk, vblk, mask):
        m, acc = state
        s = _dot_nt(q_ref[0, :, qs[h]], kblk)
        if mask is not None:
            s = jnp.where(mask, s, -jnp.inf)
        m_new = jnp.maximum(m, jnp.max(s, axis=-1, keepdims=True))
        p = jnp.exp(s - m_new).astype(BF16)
        return m_new, jnp.exp(m - m_new) * acc + _dot(p, vblk)

    meta_mask = lax.broadcasted_iota(I32, (tq, LANES), 1) < N_META
    init = (jnp.full((tq, 1), -jnp.inf, F32), jnp.zeros((tq, MLA_V_EXT), F32))
    states = tuple(update(h, init, km_ref[:, qs[h]], vm_ref[:, vs[h]], meta_mask) for h in range(MLA_HEADS))

    def body(j, states):
        r0 = pl.multiple_of(j * tk, tk)
        return tuple(update(h, states[h], k_ref[0, pl.ds(r0, tk), qs[h]], v_ref[0, pl.ds(r0, tk), vs[h]], None)
                     for h in range(MLA_HEADS))

    nfull = (i * tq) // tk
    states = lax.fori_loop(0, nfull, body, states)
    r0 = pl.multiple_of(nfull * tk, tk)
    q_chunk = (i * tq + lax.broadcasted_iota(I32, (tq, tk), 0)) // CHUNK
    k_chunk = (r0 + lax.broadcasted_iota(I32, (tq, tk), 1)) // CHUNK
    last_mask = k_chunk <= q_chunk
    for h in range(MLA_HEADS):
        _, acc = update(h, states[h], k_ref[0, pl.ds(r0, tk), qs[h]], v_ref[0, pl.ds(r0, tk), vs[h]], last_mask)
        o_ref[0, :, h * MLA_V:(h + 1) * MLA_V] = (acc[:, :MLA_V] / acc[:, MLA_V:]).astype(o_ref.dtype)


def _attention(q, k, v, k_meta, v_meta, tq, tk):
    b, r, _ = q.shape
    assert tk % tq == 0 and r % tk == 0
    return pl.pallas_call(
        functools.partial(_attn_kernel, tq=tq, tk=tk),
        grid=(b, r // tq),
        in_specs=[
            pl.BlockSpec((1, tq, q.shape[2]), lambda bi, i: (bi, i, 0)),
            pl.BlockSpec((1, r, k.shape[2]), lambda bi, i: (bi, 0, 0)),
            pl.BlockSpec((1, r, v.shape[2]), lambda bi, i: (bi, 0, 0)),
            pl.BlockSpec(k_meta.shape, lambda bi, i: (0, 0)),
            pl.BlockSpec(v_meta.shape, lambda bi, i: (0, 0)),
        ],
        out_specs=pl.BlockSpec((1, tq, MLA_HEADS * MLA_V), lambda bi, i: (bi, i, 0)),
        out_shape=jax.ShapeDtypeStruct((b, r, MLA_HEADS * MLA_V), BF16),
        compiler_params=pltpu.CompilerParams(vmem_limit_bytes=VMEM_LIMIT),
        name="mla_attention",
    )(q, k, v, k_meta, v_meta)


def _gla_level_consts(c):
    idx = np.arange(c)
    tril = (idx[None, :] <= idx[:, None]).astype(np.float32)
    sels, masks = [], []
    m = c // 2
    while m >= 1:
        ref = (idx // (2 * m)) * (2 * m) + m - 1
        sels.append(tril[ref])
        same = (idx[:, None] // (2 * m)) == (idx[None, :] // (2 * m))
        masks.append(same & ((idx[:, None] % (2 * m)) >= m) & ((idx[None, :] % (2 * m)) < m))
        m //= 2
    return tril, np.stack(sels), np.stack(masks).astype(np.float32)


def _gla_kernel(gq_ref, gk_ref, gv_ref, gg_ref, la_ref, s0_ref, gn_ref, tril_ref, sel_ref, msk_ref,
                y_ref, sout_ref, s_scr, *, c, nlev):
    ci = pl.program_id(1)

    @pl.when(ci == 0)
    def _():
        s_scr[...] = s0_ref[...]

    la = la_ref[...]
    hi = la.astype(BF16)
    rest = la - hi.astype(F32)
    mid = rest.astype(BF16)
    lo = (rest - mid.astype(F32)).astype(BF16)
    csel = jnp.concatenate([tril_ref[...]] + [sel_ref[lv] for lv in range(nlev)], axis=0).astype(BF16)
    cum = _dot(csel, hi) + _dot(csel, mid) + _dot(csel, lo)
    b_all = cum[:c]
    brefs = [cum[(lv + 1) * c:(lv + 2) * c] for lv in range(nlev)]
    eye = (lax.broadcasted_iota(I32, (c, c), 0) == lax.broadcasted_iota(I32, (c, c), 1))
    eye_k = (lax.broadcasted_iota(I32, (GLA_HK, GLA_HK), 0) == lax.broadcasted_iota(I32, (GLA_HK, GLA_HK), 1))

    for h in range(GLA_HEADS):
        ks = slice(h * GLA_HK, (h + 1) * GLA_HK)
        vs = slice(h * GLA_HV, (h + 1) * GLA_HV)
        q = gq_ref[:, ks].astype(F32) * (GLA_HK ** -0.5)
        k = gk_ref[:, ks].astype(F32)
        v = gv_ref[:, vs]
        bh = b_all[:, ks]
        s_old = s_scr[h]

        o = _dot((q * jnp.exp(bh)).astype(BF16), s_old.astype(BF16))
        a = jnp.where(eye, _dot_nt(q.astype(BF16), k.astype(BF16)), 0.0)
        for lv in range(nlev):
            br = brefs[lv][:, ks]
            qd = q * jnp.exp(jnp.minimum(bh - br, 0.0))
            kd = k * jnp.exp(jnp.minimum(br - bh, 0.0))
            a = a + msk_ref[lv] * _dot_nt(qd.astype(BF16), kd.astype(BF16))
        o = o + _dot(a.astype(BF16), v)

        b_last = bh[c - 1:c, :]
        kdec = k * jnp.exp(b_last - bh)
        dec_col = jnp.sum(jnp.where(eye_k, jnp.exp(b_last), 0.0), axis=1, keepdims=True)
        s_scr[h] = dec_col * s_old + _dot_tn(kdec.astype(BF16), v)

        gg = gg_ref[:, vs].astype(F32)
        silu = gg / (1.0 + jnp.exp(-gg))
        y_ref[:, vs] = (_rms(o, gn_ref[...]) * silu).astype(y_ref.dtype)

    @pl.when(ci == pl.num_programs(1) - 1)
    def _():
        sout_ref[0] = s_scr[...]


def _gla(proj, log_a, s0, g_norm, nb, c):
    m = proj.shape[0]
    nc = m // nb // c
    tril, sels, masks = _gla_level_consts(c)
    nlev = sels.shape[0]
    row = lambda bi, ci: bi * nc + ci
    return pl.pallas_call(
        functools.partial(_gla_kernel, c=c, nlev=nlev),
        grid=(nb, nc),
        in_specs=[
            pl.BlockSpec((c, GLA_DK), lambda bi, ci: (row(bi, ci), COL_GQ // GLA_DK)),
            pl.BlockSpec((c, GLA_DK), lambda bi, ci: (row(bi, ci), COL_GK // GLA_DK)),
            pl.BlockSpec((c, GLA_DV), lambda bi, ci: (row(bi, ci), COL_GV // GLA_DV)),
            pl.BlockSpec((c, GLA_DV), lambda bi, ci: (row(bi, ci), COL_GG // GLA_DV)),
            pl.BlockSpec((c, GLA_DK), lambda bi, ci: (row(bi, ci), 0)),
            pl.BlockSpec(s0.shape, lambda bi, ci: (0, 0, 0)),
            pl.BlockSpec(g_norm.shape, lambda bi, ci: (0, 0)),
            pl.BlockSpec(tril.shape, lambda bi, ci: (0, 0)),
            pl.BlockSpec(sels.shape, lambda bi, ci: (0, 0, 0)),
            pl.BlockSpec(masks.shape, lambda bi, ci: (0, 0, 0)),
        ],
        out_specs=[
            pl.BlockSpec((c, GLA_DV), lambda bi, ci: (row(bi, ci), 0)),
            pl.BlockSpec((1,) + s0.shape, lambda bi, ci: (bi, 0, 0, 0)),
        ],
        out_shape=[
            jax.ShapeDtypeStruct((m, GLA_DV), BF16),
            jax.ShapeDtypeStruct((nb,) + s0.shape, F32),
        ],
        scratch_shapes=[pltpu.VMEM(s0.shape, F32)],
        compiler_params=pltpu.CompilerParams(vmem_limit_bytes=VMEM_LIMIT),
        name="gla",
    )(proj, proj, proj, proj, log_a, s0, g_norm, jnp.asarray(tril), jnp.asarray(sels), jnp.asarray(masks))


def _mix_kernel(ya_ref, yb_ref, ga_ref, gb_ref, x_ref, wa_ref, wb_ref, wo_ref, gf_ref, wq_ref, keys_ref,
                h1_ref, hn_ref, st_ref):
    sig = lambda t: 1.0 / (1.0 + jnp.exp(-t))
    a = _dot(ya_ref[...], wa_ref[...])
    b = _dot(yb_ref[...], wb_ref[...])
    mix = sig(ga_ref[...].astype(F32)) * a + sig(gb_ref[...].astype(F32)) * b
    h1 = x_ref[...] + _dot(mix.astype(BF16), wo_ref[...])
    h1_ref[...] = h1
    hn = _rms(h1, gf_ref[...]).astype(BF16)
    hn_ref[...] = hn
    qp = _dot(hn, wq_ref[...]).astype(BF16)
    for hp in range(2 * PEER_HEADS):
        st_ref[hp] = _dot_nt(keys_ref[hp % 2], qp[:, hp * PEER_HALF:(hp + 1) * PEER_HALF])


def _mix(y_a, y_b, proj, x2d, w_a, w_b, w_o, g_ffn, w_q, keys, tm):
    m = x2d.shape[0]
    full = lambda a: pl.BlockSpec(a.shape, lambda i: (0,) * a.ndim)
    return pl.pallas_call(
        _mix_kernel,
        grid=(m // tm,),
        in_specs=[
            pl.BlockSpec((tm, D_MODEL), lambda i: (i, 0)),
            pl.BlockSpec((tm, D_MODEL), lambda i: (i, 0)),
            pl.BlockSpec((tm, D_MODEL), lambda i: (i, COL_GATE_A // D_MODEL)),
            pl.BlockSpec((tm, D_MODEL), lambda i: (i, COL_GATE_B // D_MODEL)),
            pl.BlockSpec((tm, D_MODEL), lambda i: (i, 0)),
            full(w_a), full(w_b), full(w_o), full(g_ffn), full(w_q), full(keys),
        ],
        out_specs=[
            pl.BlockSpec((tm, D_MODEL), lambda i: (i, 0)),
            pl.BlockSpec((tm, D_MODEL), lambda i: (i, 0)),
            pl.BlockSpec((2 * PEER_HEADS, PEER_NKEYS, tm), lambda i: (0, 0, i)),
        ],
        out_shape=[
            jax.ShapeDtypeStruct((m, D_MODEL), F32),
            jax.ShapeDtypeStruct((m, D_MODEL), BF16),
            jax.ShapeDtypeStruct((2 * PEER_HEADS, PEER_NKEYS, m), F32),
        ],
        compiler_params=pltpu.CompilerParams(vmem_limit_bytes=VMEM_LIMIT),
        name="mix_out_peer_scores",
    )(y_a, y_b, proj, proj, x2d, w_a, w_b, w_o, g_ffn, w_q, keys)


def _topk_rows(s, k, n):
    rid = lax.broadcasted_iota(I32, s.shape, 0).astype(F32)
    vals, ids = [], []
    for _ in range(k):
        m = jnp.max(s, axis=0, keepdims=True)
        am = jnp.min(jnp.where(s == m, rid, float(n)), axis=0, keepdims=True)
        vals.append(m)
        ids.append(am)
        s = jnp.where(rid == am, -jnp.inf, s)
    return jnp.concatenate(vals, axis=0), jnp.concatenate(ids, axis=0)


def _topk_kernel(st_ref, idx_out_ref, gate_out_ref, idx_ref, gate_ref):
    def head(h, carry):
        v1, i1 = _topk_rows(st_ref[2 * h], PEER_TOPK, PEER_NKEYS)
        v2, i2 = _topk_rows(st_ref[2 * h + 1], PEER_TOPK, PEER_NKEYS)
        half = PEER_TOPK // 2
        sel = [(slice(0, 1), slice(None))] + [(slice(a, a + 1), slice(0, half)) for a in range(1, half)]
        sel.append((slice(half, None), slice(0, 1)))
        cs = jnp.concatenate([v1[sa] + v2[sb] for sa, sb in sel], axis=0)
        ci = jnp.concatenate([i1[sa] * PEER_NKEYS + i2[sb] for sa, sb in sel], axis=0)
        n = PEER_TOPK * PEER_TOPK
        r = lax.broadcasted_iota(I32, cs.shape, 0)
        mid = PEER_TOPK + (r - PEER_TOPK) // half * PEER_TOPK + (r - PEER_TOPK) % half
        tail = (r - (PEER_TOPK + (half - 1) * half) + half) * PEER_TOPK
        pos = jnp.where(r < PEER_TOPK, r, jnp.where(r < PEER_TOPK + (half - 1) * half, mid, tail))
        pos = pos.astype(F32)
        vals, ids = [], []
        for _ in range(PEER_TOPK):
            m = jnp.max(cs, axis=0, keepdims=True)
            ap = jnp.min(jnp.where(cs == m, pos, float(n)), axis=0, keepdims=True)
            sel = pos == ap
            vals.append(m)
            ids.append(jnp.sum(jnp.where(sel, ci, 0.0), axis=0, keepdims=True))
            cs = jnp.where(sel, -jnp.inf, cs)
        best = jnp.concatenate(vals, axis=0)
        e = jnp.exp(best - best[0:1])
        gate_ref[h] = e / jnp.sum(e, axis=0, keepdims=True)
        idx_ref[h] = jnp.concatenate(ids, axis=0).astype(I32)
        return carry

    lax.fori_loop(0, PEER_HEADS, head, 0)
    tt = idx_ref.shape[2]
    idx_out_ref[...] = idx_ref[...].reshape(PEER_PAIRS, tt).T * ROW_WORDS
    gate_out_ref[...] = gate_ref[...].reshape(PEER_PAIRS, tt).T


def _topk(st, tt):
    t = st.shape[2]
    return pl.pallas_call(
        _topk_kernel,
        grid=(t // tt,),
        in_specs=[pl.BlockSpec((2 * PEER_HEADS, PEER_NKEYS, tt), lambda i: (0, 0, i))],
        out_specs=[
            pl.BlockSpec((tt, PEER_PAIRS), lambda i: (i, 0)),
            pl.BlockSpec((tt, PEER_PAIRS), lambda i: (i, 0)),
        ],
        out_shape=[
            jax.ShapeDtypeStruct((t, PEER_PAIRS), I32),
            jax.ShapeDtypeStruct((t, PEER_PAIRS), F32),
        ],
        scratch_shapes=[
            pltpu.VMEM((PEER_HEADS, PEER_TOPK, tt), I32),
            pltpu.VMEM((PEER_HEADS, PEER_TOPK, tt), F32),
        ],
        compiler_params=pltpu.CompilerParams(vmem_limit_bytes=VMEM_LIMIT),
        name="peer_topk",
    )(st)


def _pack_kernel(x_ref, o_ref):
    rows = x_ref.shape[0]
    x = x_ref[...].astype(F32).reshape(rows * ROW_BF16, LANES)
    o_ref[...] = pltpu.bitcast(x.astype(BF16), I32)


def _pack_rows(a, rows):
    n = a.shape[0]
    return pl.pallas_call(
        _pack_kernel,
        grid=(n // rows,),
        in_specs=[pl.BlockSpec((rows, a.shape[1]), lambda i: (i, 0))],
        out_specs=pl.BlockSpec((rows * ROW_WORDS, LANES), lambda i: (i, 0)),
        out_shape=jax.ShapeDtypeStruct((n * ROW_WORDS, LANES), I32),
        compiler_params=pltpu.CompilerParams(vmem_limit_bytes=VMEM_LIMIT),
        name="pack_rows",
    )(a)


def _lane_group_consts():
    lane = np.arange(PEER_PAIRS * ROW_BF16)
    diag = (lane[None, :] % ROW_BF16) == np.arange(ROW_BF16)[:, None]
    group = (lane[:, None] // ROW_BF16) == np.arange(PEER_PAIRS)[None, :]
    return jnp.asarray(diag, F32), jnp.asarray(group, BF16), jnp.asarray(group.T, BF16)


GROUP = 8
TOKEN_WORDS = PEER_PAIRS * ROW_WORDS
GROUP_IDX = GROUP * PEER_PAIRS
NWIN = 4


def _gather_group(tab_v, win, stage):
    for j in range(GROUP_IDX):
        e = pl.multiple_of(win[j], ROW_WORDS)
        stage[j * ROW_WORDS:(j + 1) * ROW_WORDS, :] = tab_v[pl.ds(e, ROW_WORDS), :]


def _token_rows(stage, j):
    return pltpu.bitcast(stage[j * TOKEN_WORDS:(j + 1) * TOKEN_WORDS, :], BF16)


def _dot_hi_lo(x, w):
    hi = x.astype(BF16).astype(F32)
    r = _dot(jnp.concatenate([hi, x - hi], axis=0).astype(BF16), w)
    return r[:GROUP] + r[GROUP:]


def _pipelined_groups(tab_hbm, idx_hbm, tab_v, wins, stages, tab_sem, win_sems, tt, compute):
    i = pl.program_id(0)
    nsteps = pl.num_programs(0)
    ng = tt // GROUP
    assert ng % NWIN == 0 and ng >= 2 * NWIN and NWIN % len(stages) == 0

    def fill(w, g):
        group = jnp.minimum(i * ng + g, nsteps * ng - 1)
        return pltpu.make_async_copy(idx_hbm.at[pl.ds(group * GROUP_IDX, GROUP_IDX)], wins[w], win_sems.at[w])

    @pl.when(i == 0)
    def _():
        table = pltpu.make_async_copy(tab_hbm, tab_v, tab_sem)
        table.start()
        for w in range(NWIN):
            fill(w, w).start()
        table.wait()

    def refill(g, u):
        fill(u % NWIN, g + NWIN).start()

    def gather(g, u):
        fill(u % NWIN, g).wait()
        _gather_group(tab_v, wins[u % NWIN], stages[u % len(stages)])

    def gather_next_and_compute(g, u):
        fill((u + 1) % NWIN, g + 1).wait()
        refill(g, u)
        _gather_group(tab_v, wins[(u + 1) % NWIN], stages[(u + 1) % len(stages)])
        compute(stages[u % len(stages)], g)

    gather(0, 0)

    def quad(k, carry):
        for u in range(NWIN):
            gather_next_and_compute(NWIN * k + u, u)
        return carry

    lax.fori_loop(0, ng // NWIN - 1, quad, 0)
    for u in range(NWIN - 1):
        gather_next_and_compute(ng - NWIN + u, u)
    compute(stages[(NWIN - 1) % len(stages)], ng - 1)
    refill(ng - 1, NWIN - 1)

    @pl.when(i == nsteps - 1)
    def _():
        for w in range(NWIN):
            fill(w, 0).wait()


def _peer_u_kernel(idx_hbm, tab_hbm, hp_ref, gate_ref, diag_ref, group_ref, coef_ref,
                   tab_v, win0, win1, win2, win3, stage_a, stage_b, zbuf, act, tab_sem, win_sems, *, tt):

    def compute(stage, g):
        for j in range(GROUP):
            r0 = pl.multiple_of((g * GROUP + j) * ROW_WORDS, ROW_WORDS)
            ht = pltpu.bitcast(hp_ref[pl.ds(r0, ROW_WORDS), :], BF16)
            z = _dot_nt(ht, _token_rows(stage, j)) * diag_ref[...]
            zbuf[j:j + 1, :] = jnp.sum(z, axis=0, keepdims=True)
        rows = pl.ds(pl.multiple_of(g * GROUP, GROUP), GROUP)
        act[rows, :] = _dot_hi_lo(zbuf[...], group_ref[...])

    _pipelined_groups(tab_hbm, idx_hbm, tab_v, (win0, win1, win2, win3), (stage_a, stage_b), tab_sem, win_sems,
                      tt, compute)
    a = act[...]
    gelu = 0.5 * a * (1.0 + lax.erf(a * (2.0 ** -0.5)))
    coef_ref[...] = gelu * gate_ref[...]


def _peer_v_kernel(idx_hbm, tab_hbm, coef_ref, h1_ref, diag_ref, group_t_ref, y_ref,
                   tab_v, win0, win1, win2, win3, stage_a, stage_b, tab_sem, win_sems, *, tt):

    def compute(stage, g):
        rows8 = pl.ds(pl.multiple_of(g * GROUP, GROUP), GROUP)
        crep = _dot_hi_lo(coef_ref[rows8, :], group_t_ref[...])
        outs = []
        for j in range(GROUP):
            c = jnp.broadcast_to(crep[j:j + 1, :], (ROW_BF16, crep.shape[1])) * diag_ref[...]
            outs.append(_dot_hi_lo(c, _token_rows(stage, j)))
        y_ref[rows8, :] = h1_ref[rows8, :] + jnp.stack(outs).reshape(GROUP, D_MODEL)

    _pipelined_groups(tab_hbm, idx_hbm, tab_v, (win0, win1, win2, win3), (stage_a, stage_b), tab_sem, win_sems,
                      tt, compute)


def _peer_scratch(tt, u_phase):
    shapes = [pltpu.VMEM((PEER_EXPERTS * ROW_WORDS, LANES), I32)]
    shapes += [pltpu.SMEM((GROUP_IDX,), I32)] * NWIN
    shapes += [pltpu.VMEM((GROUP * TOKEN_WORDS, LANES), I32)] * 2
    if u_phase:
        shapes += [pltpu.VMEM((GROUP, PEER_PAIRS * ROW_BF16), F32), pltpu.VMEM((tt, PEER_PAIRS), F32)]
    return shapes + [pltpu.SemaphoreType.DMA, pltpu.SemaphoreType.DMA((NWIN,))]


def _peer_u(idx_flat, tab, hn_packed, gate, tt):
    t = gate.shape[0]
    diag, group, _ = _lane_group_consts()
    any_spec = pl.BlockSpec(memory_space=pl.ANY)
    return pl.pallas_call(
        functools.partial(_peer_u_kernel, tt=tt),
        grid=(t // tt,),
        in_specs=[
            any_spec, any_spec,
            pl.BlockSpec((tt * ROW_WORDS, LANES), lambda i: (i, 0)),
            pl.BlockSpec((tt, PEER_PAIRS), lambda i: (i, 0)),
            pl.BlockSpec(diag.shape, lambda i: (0, 0)),
            pl.BlockSpec(group.shape, lambda i: (0, 0)),
        ],
        out_specs=pl.BlockSpec((tt, PEER_PAIRS), lambda i: (i, 0)),
        out_shape=jax.ShapeDtypeStruct((t, PEER_PAIRS), F32),
        scratch_shapes=_peer_scratch(tt, True),
        compiler_params=pltpu.CompilerParams(vmem_limit_bytes=VMEM_LIMIT),
        name="peer_u",
    )(idx_flat, tab, hn_packed, gate, diag, group)


def _peer_v(idx_flat, tab, coef, h1_rows, tt):
    t = coef.shape[0]
    diag, _, group_t = _lane_group_consts()
    any_spec = pl.BlockSpec(memory_space=pl.ANY)
    return pl.pallas_call(
        functools.partial(_peer_v_kernel, tt=tt),
        grid=(t // tt,),
        in_specs=[
            any_spec, any_spec,
            pl.BlockSpec((tt, PEER_PAIRS), lambda i: (i, 0)),
            pl.BlockSpec((tt, D_MODEL), lambda i: (i, 0)),
            pl.BlockSpec(diag.shape, lambda i: (0, 0)),
            pl.BlockSpec(group_t.shape, lambda i: (0, 0)),
        ],
        out_specs=pl.BlockSpec((tt, D_MODEL), lambda i: (i, 0)),
        out_shape=jax.ShapeDtypeStruct((t, D_MODEL), F32),
        scratch_shapes=_peer_scratch(tt, False),
        compiler_params=pltpu.CompilerParams(vmem_limit_bytes=VMEM_LIMIT),
        name="peer_v",
    )(idx_flat, tab, coef, h1_rows, diag, group_t)


def _rope_tables(pos):
    inv_freq = ROPE_THETA ** (-jnp.arange(ROPE_HALF, dtype=F32) / ROPE_HALF)
    ang = pos.astype(F32)[:, None] * inv_freq[None, :]
    cos, sin = jnp.cos(ang), jnp.sin(ang)
    cos_t = jnp.tile(cos, (1, LANES // ROPE_HALF))
    sin_t = jnp.tile(jnp.concatenate([-sin, sin], axis=1), (1, LANES // MLA_ROPE))
    return cos_t, sin_t


def _pack_in_proj(w_in):
    offs = np.cumsum([0, MLA_Q_RANK, MLA_KV_RANK, MLA_ROPE, GLA_DK, GLA_DK, GLA_DV, GLA_GATE_RANK, GLA_DV,
                      D_MODEL, D_MODEL])
    part = lambda i: w_in[:, int(offs[i]):int(offs[i + 1])]
    zeros = lambda n: jnp.zeros((D_MODEL, n), w_in.dtype)
    cols = [part(0), part(1), part(2), part(6), zeros(LANES - MLA_ROPE - GLA_GATE_RANK), zeros(COL_GQ - COL_KPE - LANES),
            part(3), part(4), part(5), part(7), part(8), part(9)]
    w = jnp.concatenate(cols, axis=1).astype(BF16)
    assert w.shape[1] == PROJ_COLS
    return w


def _head_major(w, widths):
    per = sum(widths)
    w = w.reshape(w.shape[0], -1, per)
    parts, o = [], 0
    for wd in widths:
        parts.append(w[:, :, o:o + wd].reshape(w.shape[0], -1))
        o += wd
    return jnp.concatenate(parts, axis=1)


def _row_tile(m, want):
    t = min(m, want)
    assert m % t == 0
    return t


def kernel(x, meta, norm_mix, w_in, mla_q_norm, mla_w_uq, mla_kv_norm, mla_w_ukv, qn_nope, qn_pe, kn_nope, kn_pe, gla_w_a2, gla_b_a, gla_norm, w_o_mla, w_o_gla, w_out, norm_ffn, peer_w_q, peer_keys, peer_u, peer_v):
    nb, r, d = x.shape
    assert d == D_MODEL and norm_mix.shape[0] == 1 and meta.shape == (N_META, D_MODEL)
    assert r % 256 == 0
    t = nb * r
    x2d = x.reshape(t, d)
    row = lambda a: a.reshape(1, -1).astype(F32)

    w1 = _pack_in_proj(w_in[0])
    g_mix = row(norm_mix[0])
    tile2 = lambda g: jnp.tile(row(g), (1, LANES // MLA_ROPE))
    gkp = jnp.concatenate([row(kn_pe[0]), jnp.ones((1, LANES - MLA_ROPE), F32)], axis=1)
    wa2 = jnp.zeros((LANES, GLA_DK), F32).at[MLA_ROPE:MLA_ROPE + GLA_GATE_RANK].set(gla_w_a2[0]).astype(BF16)
    prep_w = (
        row(mla_q_norm[0]), _head_major(mla_w_uq[0], (MLA_NOPE, MLA_ROPE)).astype(BF16),
        row(mla_kv_norm[0]), _head_major(mla_w_ukv[0], (MLA_NOPE, MLA_V)).astype(BF16),
        row(qn_nope[0]), tile2(qn_pe[0]), row(kn_nope[0]), gkp, wa2, row(gla_b_a[0]),
    )
    g_gla = row(gla_norm[0])

    proj_m = _inproj(meta.astype(F32), g_mix, w1, N_META, PROJ_COLS // 2)
    cos_m, sin_m = _rope_tables(jnp.arange(N_META))
    _, k_m, v_m, la_m = _prep(proj_m, cos_m, sin_m, prep_w, N_META, N_META)
    s_zero = jnp.zeros((GLA_HEADS, GLA_HK, GLA_HV), F32)
    _, s_meta = _gla(proj_m, la_m, s_zero, g_gla, 1, N_META)
    pad = ((0, LANES - N_META), (0, 0))
    k_m, v_m = jnp.pad(k_m, pad), jnp.pad(v_m, pad)

    proj = _inproj(x2d, g_mix, w1, _row_tile(t, 512), PROJ_COLS // 2)
    cos_t, sin_t = _rope_tables(N_META + jnp.arange(r))
    q, k, v, log_a = _prep(proj, cos_t, sin_t, prep_w, 256, r)
    y_a = _attention(q.reshape(nb, r, -1), k.reshape(nb, r, -1), v.reshape(nb, r, -1), k_m, v_m, 256, 512)
    y_b, _ = _gla(proj, log_a, s_meta[0], g_gla, nb, CHUNK)

    h1, hn, st = _mix(y_a.reshape(t, -1), y_b, proj, x2d, w_o_mla[0].astype(BF16), w_o_gla[0].astype(BF16),
                      w_out[0].astype(BF16), row(norm_ffn[0]), peer_w_q[0].astype(BF16),
                      peer_keys[0].astype(BF16), _row_tile(t, 512))
    idx, gate = _topk(st, 256)
    idx = idx.reshape(-1)

    tt = 128
    coef = _peer_u(idx, _pack_rows(peer_u[0], 512), _pack_rows(hn, 512), gate, tt)
    y = _peer_v(idx, _pack_rows(peer_v[0], 512), coef, h1, tt)
    return y.reshape(nb, r, d)
```

```python
import functools
import math

import jax
import jax.numpy as jnp
import numpy as np
from jax import lax
from jax.experimental import pallas as pl
from jax.experimental.pallas import tpu as pltpu

F32 = jnp.float32
BF16 = jnp.bfloat16
I32 = jnp.int32

D_MODEL = 1024
CHUNK = 64
N_META = 16
EPS = 1e-6

MLA_HEADS = 8
MLA_Q_RANK = 384
MLA_KV_RANK = 256
MLA_NOPE = 128
MLA_ROPE = 64
MLA_V = 128
ROPE_HALF = MLA_ROPE // 2
ROPE_THETA = 10000.0
MLA_QK_PAD = 256
MLA_V_EXT = 256

GLA_HEADS = 4
GLA_DK = D_MODEL // 2
GLA_DV = D_MODEL
GLA_HK = GLA_DK // GLA_HEADS
GLA_HV = GLA_DV // GLA_HEADS
GLA_GATE_RANK = 16
GLA_TAU = 16.0

PEER_HEADS = 8
PEER_NKEYS = 128
PEER_EXPERTS = PEER_NKEYS * PEER_NKEYS
PEER_HALF = 128
PEER_TOPK = 16
PEER_PAIRS = PEER_HEADS * PEER_TOPK

LANES = 128
ROW_WORDS = 4
ROW_BF16 = 8
VMEM_LIMIT = 56 * 1024 * 1024
COL_CQ = 0
COL_CKV = 384
COL_KPE = 640
COL_GQ = 1024
COL_GK = 1536
COL_GV = 2048
COL_GG = 3072
COL_GATE_A = 4096
COL_GATE_B = 5120
PROJ_COLS = 6144


def _dot(a, b):
    return jnp.dot(a, b, preferred_element_type=F32)


def _dot_nt(a, b):
    return lax.dot_general(a, b, (((1,), (1,)), ((), ())), preferred_element_type=F32)


def _dot_tn(a, b):
    return lax.dot_general(a, b, (((0,), (0,)), ((), ())), preferred_element_type=F32)


def _rms(x, g):
    ms = jnp.mean(x * x, axis=-1, keepdims=True)
    return x * lax.rsqrt(ms + EPS) * g


def _split_bf16(x):
    hi = x.astype(BF16)
    lo = (x - hi.astype(F32)).astype(BF16)
    return hi, lo


def _inproj_kernel(x_ref, g_ref, w_ref, o_ref):
    n = _rms(x_ref[...], g_ref[...]).astype(BF16)
    o_ref[...] = _dot(n, w_ref[...]).astype(o_ref.dtype)


def _inproj(x2d, g, w, tm, tn):
    m, d = x2d.shape
    n = w.shape[1]
    return pl.pallas_call(
        _inproj_kernel,
        grid=(n // tn, m // tm),
        in_specs=[
            pl.BlockSpec((tm, d), lambda j, i: (i, 0)),
            pl.BlockSpec((1, d), lambda j, i: (0, 0)),
            pl.BlockSpec((d, tn), lambda j, i: (0, j)),
        ],
        out_specs=pl.BlockSpec((tm, tn), lambda j, i: (i, j)),
        out_shape=jax.ShapeDtypeStruct((m, n), BF16),
        compiler_params=pltpu.CompilerParams(vmem_limit_bytes=VMEM_LIMIT),
        name="inproj",
    )(x2d, g, w)


def _rope_pair_tile(xp, g_tile, cos, sin, lane):
    sq = xp * xp
    lo = lane < MLA_ROPE
    s_lo = jnp.sum(jnp.where(lo, sq, 0.0), axis=-1, keepdims=True)
    s_hi = jnp.sum(jnp.where(lo, 0.0, sq), axis=-1, keepdims=True)
    ms = jnp.where(lo, s_lo, s_hi) * (1.0 / MLA_ROPE)
    y = xp * lax.rsqrt(ms + EPS) * g_tile
    up = pltpu.roll(y, LANES - ROPE_HALF, axis=1)
    dn = pltpu.roll(y, ROPE_HALF, axis=1)
    partner = jnp.where((lane % MLA_ROPE) < ROPE_HALF, up, dn)
    return y * cos + partner * sin


def _prep_kernel(p_ref, cos_ref, sin_ref, qn_ref, wuq_ref, kvn_ref, wukv_ref, gqn_ref, gqp_ref,
                 gkn_ref, gkp_ref, wa2_ref, ba_ref, q_ref, k_ref, v_ref, la_ref):
    tm = p_ref.shape[0]
    lane = lax.broadcasted_iota(I32, (tm, LANES), 1)
    cos = cos_ref[...]
    sin = sin_ref[...]
    scale = (MLA_NOPE + MLA_ROPE) ** -0.5

    cq = p_ref[:, COL_CQ:COL_CQ + MLA_Q_RANK].astype(F32)
    qa = _dot(_rms(cq, qn_ref[...]).astype(BF16), wuq_ref[...])
    ckv = p_ref[:, COL_CKV:COL_CKV + MLA_KV_RANK].astype(F32)
    kva = _dot(_rms(ckv, kvn_ref[...]).astype(BF16), wukv_ref[...])

    misc = p_ref[:, COL_KPE:COL_KPE + LANES]
    kpe = _rope_pair_tile(misc.astype(F32), gkp_ref[...], cos, sin, lane)
    kpe = jnp.where(lane < MLA_ROPE, kpe, 0.0).astype(BF16)

    pe_off = MLA_HEADS * MLA_NOPE
    for j in range(MLA_HEADS // 2):
        t = _rope_pair_tile(qa[:, pe_off + j * LANES: pe_off + (j + 1) * LANES], gqp_ref[...], cos, sin, lane)
        t = t * scale
        even = jnp.where(lane < MLA_ROPE, t, 0.0)
        odd = jnp.where(lane < MLA_ROPE, pltpu.roll(t, MLA_ROPE, axis=1), 0.0)
        for h, pe in ((2 * j, even), (2 * j + 1, odd)):
            qn = _rms(qa[:, h * MLA_NOPE:(h + 1) * MLA_NOPE], gqn_ref[...]) * scale
            q_ref[:, h * MLA_QK_PAD: h * MLA_QK_PAD + MLA_NOPE] = qn.astype(BF16)
            q_ref[:, h * MLA_QK_PAD + MLA_NOPE:(h + 1) * MLA_QK_PAD] = pe.astype(BF16)
    for h in range(MLA_HEADS):
        kn = _rms(kva[:, h * MLA_NOPE:(h + 1) * MLA_NOPE], gkn_ref[...])
        k_ref[:, h * MLA_QK_PAD: h * MLA_QK_PAD + MLA_NOPE] = kn.astype(BF16)
        k_ref[:, h * MLA_QK_PAD + MLA_NOPE:(h + 1) * MLA_QK_PAD] = kpe
    v_off = MLA_HEADS * MLA_NOPE
    for h in range(MLA_HEADS):
        v_ref[:, h * MLA_V_EXT: h * MLA_V_EXT + MLA_V] = kva[:, v_off + h * MLA_V: v_off + (h + 1) * MLA_V].astype(BF16)
        v_ref[:, h * MLA_V_EXT + MLA_V:(h + 1) * MLA_V_EXT] = jnp.ones((tm, MLA_V_EXT - MLA_V), BF16)

    z = _dot(misc, wa2_ref[...]) + ba_ref[...]
    log_sig = jnp.minimum(z, 0.0) - jnp.log(1.0 + jnp.exp(-jnp.abs(z)))
    la_ref[...] = log_sig * (1.0 / GLA_TAU)


def _prep(proj, cos_t, sin_t, wts, tm, rows_per_seq):
    m = proj.shape[0]
    nblk = rows_per_seq // tm
    full = lambda a: pl.BlockSpec(a.shape, lambda i: (0,) * a.ndim)
    return pl.pallas_call(
        _prep_kernel,
        grid=(m // tm,),
        in_specs=[
            pl.BlockSpec((tm, 1024), lambda i: (i, 0)),
            pl.BlockSpec((tm, LANES), lambda i: (i % nblk, 0)),
            pl.BlockSpec((tm, LANES), lambda i: (i % nblk, 0)),
        ] + [full(w) for w in wts],
        out_specs=[
            pl.BlockSpec((tm, MLA_HEADS * MLA_QK_PAD), lambda i: (i, 0)),
            pl.BlockSpec((tm, MLA_HEADS * MLA_QK_PAD), lambda i: (i, 0)),
            pl.BlockSpec((tm, MLA_HEADS * MLA_V_EXT), lambda i: (i, 0)),
            pl.BlockSpec((tm, GLA_DK), lambda i: (i, 0)),
        ],
        out_shape=[
            jax.ShapeDtypeStruct((m, MLA_HEADS * MLA_QK_PAD), BF16),
            jax.ShapeDtypeStruct((m, MLA_HEADS * MLA_QK_PAD), BF16),
            jax.ShapeDtypeStruct((m, MLA_HEADS * MLA_V_EXT), BF16),
            jax.ShapeDtypeStruct((m, GLA_DK), F32),
        ],
        compiler_params=pltpu.CompilerParams(vmem_limit_bytes=VMEM_LIMIT),
        name="mla_gla_prep",
    )(proj, cos_t, sin_t, *wts)


def _attn_kernel(q_ref, k_ref, v_ref, km_ref, vm_ref, o_ref, *, tq, tk):
    i = pl.program_id(1)
    qs = [slice(h * MLA_QK_PAD, (h + 1) * MLA_QK_PAD) for h in range(MLA_HEADS)]
    vs = [slice(h * MLA_V_EXT, (h + 1) * MLA_V_EXT) for h in range(MLA_HEADS)]

    def update(h, state, kblk, vblk, mask):
        m, acc = state
        s = _dot_nt(q_ref[0, :, qs[h]], kblk)
        if mask is not None:
            s = jnp.where(mask, s, -jnp.inf)
        m_new = jnp.maximum(m, jnp.max(s, axis=-1, keepdims=True))
        p = jnp.exp(s - m_new).astype(BF16)
        return m_new, jnp.exp(m - m_new) * acc + _dot(p, vblk)

    meta_mask = lax.broadcasted_iota(I32, (tq, LANES), 1) < N_META
    init = (jnp.full((tq, 1), -jnp.inf, F32), jnp.zeros((tq, MLA_V_EXT), F32))
    states = tuple(update(h, init, km_ref[:, qs[h]], vm_ref[:, vs[h]], meta_mask) for h in range(MLA_HEADS))

    def body(j, states):
        r0 = pl.multiple_of(j * tk, tk)
        return tuple(update(h, states[h], k_ref[0, pl.ds(r0, tk), qs[h]], v_ref[0, pl.ds(r0, tk), vs[h]], None)
                     for h in range(MLA_HEADS))

    nfull = (i * tq) // tk
    states = lax.fori_loop(0, nfull, body, states)
    r0 = pl.multiple_of(nfull * tk, tk)
    q_chunk = (i * tq + lax.broadcasted_iota(I32, (tq, tk), 0)) // CHUNK
    k_chunk = (r0 + lax.broadcasted_iota(I32, (tq, tk), 1)) // CHUNK
    last_mask = k_chunk <= q_chunk
    for h in range(MLA_HEADS):
        _, acc = update(h, states[h], k_ref[0, pl.ds(r0, tk), qs[h]], v_ref[0, pl.ds(r0, tk), vs[h]], last_mask)
        o_ref[0, :, h * MLA_V:(h + 1) * MLA_V] = (acc[:, :MLA_V] / acc[:, MLA_V:]).astype(o_ref.dtype)


def _attention(q, k, v, k_meta, v_meta, tq, tk):
    b, r, _ = q.shape
    assert tk % tq == 0 and r % tk == 0
    return pl.pallas_call(
        functools.partial(_attn_kernel, tq=tq, tk=tk),
        grid=(b, r // tq),
        in_specs=[
            pl.BlockSpec((1, tq, q.shape[2]), lambda bi, i: (bi, i, 0)),
            pl.BlockSpec((1, r, k.shape[2]), lambda bi, i: (bi, 0, 0)),
            pl.BlockSpec((1, r, v.shape[2]), lambda bi, i: (bi, 0, 0)),
            pl.BlockSpec(k_meta.shape, lambda bi, i: (0, 0)),
            pl.BlockSpec(v_meta.shape, lambda bi, i: (0, 0)),
        ],
        out_specs=pl.BlockSpec((1, tq, MLA_HEADS * MLA_V), lambda bi, i: (bi, i, 0)),
        out_shape=jax.ShapeDtypeStruct((b, r, MLA_HEADS * MLA_V), BF16),
        compiler_params=pltpu.CompilerParams(vmem_limit_bytes=VMEM_LIMIT),
        name="mla_attention",
    )(q, k, v, k_meta, v_meta)


def _gla_level_consts(c):
    idx = np.arange(c)
    tril = (idx[None, :] <= idx[:, None]).astype(np.float32)
    sels, masks = [], []
    m = c // 2
    while m >= 1:
        ref = (idx // (2 * m)) * (2 * m) + m - 1
        sels.append(tril[ref])
        same = (idx[:, None] // (2 * m)) == (idx[None, :] // (2 * m))
        masks.append(same & ((idx[:, None] % (2 * m)) >= m) & ((idx[None, :] % (2 * m)) < m))
        m //= 2
    return tril, np.stack(sels), np.stack(masks).astype(np.float32)


def _gla_kernel(gq_ref, gk_ref, gv_ref, gg_ref, la_ref, s0_ref, gn_ref, tril_ref, sel_ref, msk_ref,
                y_ref, sout_ref, s_scr, *, c, nlev):
    ci = pl.program_id(1)

    @pl.when(ci == 0)
    def _():
        s_scr[...] = s0_ref[...]

    la = la_ref[...]
    hi = la.astype(BF16)
    rest = la - hi.astype(F32)
    mid = rest.astype(BF16)
    lo = (rest - mid.astype(F32)).astype(BF16)
    csel = jnp.concatenate([tril_ref[...]] + [sel_ref[lv] for lv in range(nlev)], axis=0).astype(BF16)
    cum = _dot(csel, hi) + _dot(csel, mid) + _dot(csel, lo)
    b_all = cum[:c]
    brefs = [cum[(lv + 1) * c:(lv + 2) * c] for lv in range(nlev)]
    eye = (lax.broadcasted_iota(I32, (c, c), 0) == lax.broadcasted_iota(I32, (c, c), 1))
    eye_k = (lax.broadcasted_iota(I32, (GLA_HK, GLA_HK), 0) == lax.broadcasted_iota(I32, (GLA_HK, GLA_HK), 1))

    for h in range(GLA_HEADS):
        ks = slice(h * GLA_HK, (h + 1) * GLA_HK)
        vs = slice(h * GLA_HV, (h + 1) * GLA_HV)
        q = gq_ref[:, ks].astype(F32) * (GLA_HK ** -0.5)
        k = gk_ref[:, ks].astype(F32)
        v = gv_ref[:, vs]
        bh = b_all[:, ks]
        s_old = s_scr[h]

        o = _dot((q * jnp.exp(bh)).astype(BF16), s_old.astype(BF16))
        a = jnp.where(eye, _dot_nt(q.astype(BF16), k.astype(BF16)), 0.0)
        for lv in range(nlev):
            br = brefs[lv][:, ks]
            qd = q * jnp.exp(jnp.minimum(bh - br, 0.0))
            kd = k * jnp.exp(jnp.minimum(br - bh, 0.0))
            a = a + msk_ref[lv] * _dot_nt(qd.astype(BF16), kd.astype(BF16))
        o = o + _dot(a.astype(BF16), v)

        b_last = bh[c - 1:c, :]
        kdec = k * jnp.exp(b_last - bh)
        dec_col = jnp.sum(jnp.where(eye_k, jnp.exp(b_last), 0.0), axis=1, keepdims=True)
        s_scr[h] = dec_col * s_old + _dot_tn(kdec.astype(BF16), v)

        gg = gg_ref[:, vs].astype(F32)
        silu = gg / (1.0 + jnp.exp(-gg))
        y_ref[:, vs] = (_rms(o, gn_ref[...]) * silu).astype(y_ref.dtype)

    @pl.when(ci == pl.num_programs(1) - 1)
    def _():
        sout_ref[0] = s_scr[...]


def _gla(proj, log_a, s0, g_norm, nb, c):
    m = proj.shape[0]
    nc = m // nb // c
    tril, sels, masks = _gla_level_consts(c)
    nlev = sels.shape[0]
    row = lambda bi, ci: bi * nc + ci
    return pl.pallas_call(
        functools.partial(_gla_kernel, c=c, nlev=nlev),
        grid=(nb, nc),
        in_specs=[
            pl.BlockSpec((c, GLA_DK), lambda bi, ci: (row(bi, ci), COL_GQ // GLA_DK)),
            pl.BlockSpec((c, GLA_DK), lambda bi, ci: (row(bi, ci), COL_GK // GLA_DK)),
            pl.BlockSpec((c, GLA_DV), lambda bi, ci: (row(bi, ci), COL_GV // GLA_DV)),
            pl.BlockSpec((c, GLA_DV), lambda bi, ci: (row(bi, ci), COL_GG // GLA_DV)),
            pl.BlockSpec((c, GLA_DK), lambda bi, ci: (row(bi, ci), 0)),
            pl.BlockSpec(s0.shape, lambda bi, ci: (0, 0, 0)),
            pl.BlockSpec(g_norm.shape, lambda bi, ci: (0, 0)),
            pl.BlockSpec(tril.shape, lambda bi, ci: (0, 0)),
            pl.BlockSpec(sels.shape, lambda bi, ci: (0, 0, 0)),
            pl.BlockSpec(masks.shape, lambda bi, ci: (0, 0, 0)),
        ],
        out_specs=[
            pl.BlockSpec((c, GLA_DV), lambda bi, ci: (row(bi, ci), 0)),
            pl.BlockSpec((1,) + s0.shape, lambda bi, ci: (bi, 0, 0, 0)),
        ],
        out_shape=[
            jax.ShapeDtypeStruct((m, GLA_DV), BF16),
            jax.ShapeDtypeStruct((nb,) + s0.shape, F32),
        ],
        scratch_shapes=[pltpu.VMEM(s0.shape, F32)],
        compiler_params=pltpu.CompilerParams(vmem_limit_bytes=VMEM_LIMIT),
        name="gla",
    )(proj, proj, proj, proj, log_a, s0, g_norm, jnp.asarray(tril), jnp.asarray(sels), jnp.asarray(masks))


def _mix_kernel(ya_ref, yb_ref, ga_ref, gb_ref, x_ref, wa_ref, wb_ref, wo_ref, gf_ref, wq_ref, keys_ref,
                h1_ref, hn_ref, st_ref):
    sig = lambda t: 1.0 / (1.0 + jnp.exp(-t))
    a = _dot(ya_ref[...], wa_ref[...])
    b = _dot(yb_ref[...], wb_ref[...])
    mix = sig(ga_ref[...].astype(F32)) * a + sig(gb_ref[...].astype(F32)) * b
    h1 = x_ref[...] + _dot(mix.astype(BF16), wo_ref[...])
    h1_ref[...] = h1
    hn = _rms(h1, gf_ref[...]).astype(BF16)
    hn_ref[...] = hn
    qp = _dot(hn, wq_ref[...]).astype(BF16)
    for hp in range(2 * PEER_HEADS):
        st_ref[hp] = _dot_nt(keys_ref[hp % 2], qp[:, hp * PEER_HALF:(hp + 1) * PEER_HALF])


def _mix(y_a, y_b, proj, x2d, w_a, w_b, w_o, g_ffn, w_q, keys, tm):
    m = x2d.shape[0]
    full = lambda a: pl.BlockSpec(a.shape, lambda i: (0,) * a.ndim)
    return pl.pallas_call(
        _mix_kernel,
        grid=(m // tm,),
        in_specs=[
            pl.BlockSpec((tm, D_MODEL), lambda i: (i, 0)),
            pl.BlockSpec((tm, D_MODEL), lambda i: (i, 0)),
            pl.BlockSpec((tm, D_MODEL), lambda i: (i, COL_GATE_A // D_MODEL)),
            pl.BlockSpec((tm, D_MODEL), lambda i: (i, COL_GATE_B // D_MODEL)),
            pl.BlockSpec((tm, D_MODEL), lambda i: (i, 0)),
            full(w_a), full(w_b), full(w_o), full(g_ffn), full(w_q), full(keys),
        ],
        out_specs=[
            pl.BlockSpec((tm, D_MODEL), lambda i: (i, 0)),
            pl.BlockSpec((tm, D_MODEL), lambda i: (i, 0)),
            pl.BlockSpec((2 * PEER_HEADS, PEER_NKEYS, tm), lambda i: (0, 0, i)),
        ],
        out_shape=[
            jax.ShapeDtypeStruct((m, D_MODEL), F32),
            jax.ShapeDtypeStruct((m, D_MODEL), BF16),
            jax.ShapeDtypeStruct((2 * PEER_HEADS, PEER_NKEYS, m), F32),
        ],
        compiler_params=pltpu.CompilerParams(vmem_limit_bytes=VMEM_LIMIT),
        name="mix_out_peer_scores",
    )(y_a, y_b, proj, proj, x2d, w_a, w_b, w_o, g_ffn, w_q, keys)


def _topk_rows(s, k, n):
    half = n // 2
    a, b = s[:half], s[half:]
    ra = lax.broadcasted_iota(I32, a.shape, 0).astype(F32)
    rb = ra + float(half)
    a_wins = a >= b
    w, l = jnp.where(a_wins, a, b), jnp.where(a_wins, b, a)
    wi, li = jnp.where(a_wins, ra, rb), jnp.where(a_wins, rb, ra)
    vals, ids = [], []
    for _ in range(k):
        m = jnp.max(w, axis=0, keepdims=True)
        am = jnp.min(jnp.where(w == m, wi, float(n)), axis=0, keepdims=True)
        vals.append(m)
        ids.append(am)
        hit = wi == am
        w, wi, l = jnp.where(hit, l, w), jnp.where(hit, li, wi), jnp.where(hit, -jnp.inf, l)
    return jnp.concatenate(vals, axis=0), jnp.concatenate(ids, axis=0)


def _topk_kernel(st_ref, idx_out_ref, gate_out_ref, idx_ref, gate_ref):
    def head(h, carry):
        v1, i1 = _topk_rows(st_ref[2 * h], PEER_TOPK, PEER_NKEYS)
        v2, i2 = _topk_rows(st_ref[2 * h + 1], PEER_TOPK, PEER_NKEYS)
        half = PEER_TOPK // 2
        sel = [(slice(0, 1), slice(None))] + [(slice(a, a + 1), slice(0, half)) for a in range(1, half)]
        sel.append((slice(half, None), slice(0, 1)))
        cs = jnp.concatenate([v1[sa] + v2[sb] for sa, sb in sel], axis=0)
        ci = jnp.concatenate([i1[sa] * PEER_NKEYS + i2[sb] for sa, sb in sel], axis=0)
        n = PEER_TOPK * PEER_TOPK
        r = lax.broadcasted_iota(I32, cs.shape, 0)
        mid = PEER_TOPK + (r - PEER_TOPK) // half * PEER_TOPK + (r - PEER_TOPK) % half
        tail = (r - (PEER_TOPK + (half - 1) * half) + half) * PEER_TOPK
        pos = jnp.where(r < PEER_TOPK, r, jnp.where(r < PEER_TOPK + (half - 1) * half, mid, tail))
        pos = pos.astype(F32)
        vals, ids = [], []
        for _ in range(PEER_TOPK):
            m = jnp.max(cs, axis=0, keepdims=True)
            ap = jnp.min(jnp.where(cs == m, pos, float(n)), axis=0, keepdims=True)
            sel = pos == ap
            vals.append(m)
            ids.append(jnp.sum(jnp.where(sel, ci, 0.0), axis=0, keepdims=True))
            cs = jnp.where(sel, -jnp.inf, cs)
        best = jnp.concatenate(vals, axis=0)
        e = jnp.exp(best - best[0:1])
        gate_ref[h] = e / jnp.sum(e, axis=0, keepdims=True)
        idx_ref[h] = jnp.concatenate(ids, axis=0).astype(I32)
        return carry

    lax.fori_loop(0, PEER_HEADS, head, 0)
    tt = idx_ref.shape[2]
    idx_out_ref[...] = idx_ref[...].reshape(PEER_PAIRS, tt).T * ROW_WORDS
    gate_out_ref[...] = gate_ref[...].reshape(PEER_PAIRS, tt).T


def _topk(st, tt):
    t = st.shape[2]
    return pl.pallas_call(
        _topk_kernel,
        grid=(t // tt,),
        in_specs=[pl.BlockSpec((2 * PEER_HEADS, PEER_NKEYS, tt), lambda i: (0, 0, i))],
        out_specs=[
            pl.BlockSpec((tt, PEER_PAIRS), lambda i: (i, 0)),
            pl.BlockSpec((tt, PEER_PAIRS), lambda i: (i, 0)),
        ],
        out_shape=[
            jax.ShapeDtypeStruct((t, PEER_PAIRS), I32),
            jax.ShapeDtypeStruct((t, PEER_PAIRS), F32),
        ],
        scratch_shapes=[
            pltpu.VMEM((PEER_HEADS, PEER_TOPK, tt), I32),
            pltpu.VMEM((PEER_HEADS, PEER_TOPK, tt), F32),
        ],
        compiler_params=pltpu.CompilerParams(vmem_limit_bytes=VMEM_LIMIT),
        name="peer_topk",
    )(st)


def _pack_kernel(x_ref, o_ref):
    rows = x_ref.shape[0]
    x = x_ref[...].astype(F32).reshape(rows * ROW_BF16, LANES)
    o_ref[...] = pltpu.bitcast(x.astype(BF16), I32)


def _pack_rows(a, rows):
    n = a.shape[0]
    return pl.pallas_call(
        _pack_kernel,
        grid=(n // rows,),
        in_specs=[pl.BlockSpec((rows, a.shape[1]), lambda i: (i, 0))],
        out_specs=pl.BlockSpec((rows * ROW_WORDS, LANES), lambda i: (i, 0)),
        out_shape=jax.ShapeDtypeStruct((n * ROW_WORDS, LANES), I32),
        compiler_params=pltpu.CompilerParams(vmem_limit_bytes=VMEM_LIMIT),
        name="pack_rows",
    )(a)


def _lane_group_consts():
    lane = np.arange(PEER_PAIRS * ROW_BF16)
    diag = (lane[None, :] % ROW_BF16) == np.arange(ROW_BF16)[:, None]
    group = (lane[:, None] // ROW_BF16) == np.arange(PEER_PAIRS)[None, :]
    return jnp.asarray(diag, F32), jnp.asarray(group, BF16), jnp.asarray(group.T, BF16)


GROUP = 8
TOKEN_WORDS = PEER_PAIRS * ROW_WORDS
GROUP_IDX = GROUP * PEER_PAIRS
NWIN = 4


def _gather_group(tab_v, win, stage):
    for j in range(GROUP_IDX):
        e = pl.multiple_of(win[j], ROW_WORDS)
        stage[j * ROW_WORDS:(j + 1) * ROW_WORDS, :] = tab_v[pl.ds(e, ROW_WORDS), :]


def _token_rows(stage, j):
    return pltpu.bitcast(stage[j * TOKEN_WORDS:(j + 1) * TOKEN_WORDS, :], BF16)


def _dot_hi_lo(x, w):
    n = x.shape[0]
    hi = x.astype(BF16).astype(F32)
    r = _dot(jnp.concatenate([hi, x - hi], axis=0).astype(BF16), w)
    return r[:n] + r[n:]


def _pipelined_groups(tab_hbm, idx_hbm, tab_v, wins, stages, tab_sem, win_sems, tt, compute):
    i = pl.program_id(0)
    nsteps = pl.num_programs(0)
    ng = tt // GROUP
    assert ng % NWIN == 0 and ng >= 2 * NWIN and NWIN % len(stages) == 0

    def fill(w, g):
        group = jnp.minimum(i * ng + g, nsteps * ng - 1)
        return pltpu.make_async_copy(idx_hbm.at[pl.ds(group * GROUP_IDX, GROUP_IDX)], wins[w], win_sems.at[w])

    @pl.when(i == 0)
    def _():
        table = pltpu.make_async_copy(tab_hbm, tab_v, tab_sem)
        table.start()
        for w in range(NWIN):
            fill(w, w).start()
        table.wait()

    def refill(g, u):
        fill(u % NWIN, g + NWIN).start()

    def gather(g, u):
        fill(u % NWIN, g).wait()
        _gather_group(tab_v, wins[u % NWIN], stages[u % len(stages)])

    def gather_next_and_compute(g, u):
        fill((u + 1) % NWIN, g + 1).wait()
        refill(g, u)
        _gather_group(tab_v, wins[(u + 1) % NWIN], stages[(u + 1) % len(stages)])
        compute(stages[u % len(stages)], g)

    gather(0, 0)

    def quad(k, carry):
        for u in range(NWIN):
            gather_next_and_compute(NWIN * k + u, u)
        return carry

    lax.fori_loop(0, ng // NWIN - 1, quad, 0)
    for u in range(NWIN - 1):
        gather_next_and_compute(ng - NWIN + u, u)
    compute(stages[(NWIN - 1) % len(stages)], ng - 1)
    refill(ng - 1, NWIN - 1)

    @pl.when(i == nsteps - 1)
    def _():
        for w in range(NWIN):
            fill(w, 0).wait()


def _peer_u_kernel(idx_hbm, tab_hbm, hp_ref, gate_ref, diag_ref, group_ref, coef_ref,
                   tab_v, win0, win1, win2, win3, stage_a, stage_b, act, tab_sem, win_sems, *, tt):

    def compute(stage, g):
        zs = []
        for j in range(GROUP):
            r0 = pl.multiple_of((g * GROUP + j) * ROW_WORDS, ROW_WORDS)
            ht = pltpu.bitcast(hp_ref[pl.ds(r0, ROW_WORDS), :], BF16)
            zs.append(_dot_nt(ht, _token_rows(stage, j)) * diag_ref[...])
        a = _dot_hi_lo(jnp.concatenate(zs, axis=0), group_ref[...])
        acts = [jnp.sum(a[j * ROW_BF16:(j + 1) * ROW_BF16], axis=0, keepdims=True) for j in range(GROUP)]
        rows = pl.ds(pl.multiple_of(g * GROUP, GROUP), GROUP)
        act[rows, :] = jnp.concatenate(acts, axis=0)

    _pipelined_groups(tab_hbm, idx_hbm, tab_v, (win0, win1, win2, win3), (stage_a, stage_b), tab_sem, win_sems,
                      tt, compute)
    a = act[...]
    gelu = 0.5 * a * (1.0 + lax.erf(a * (2.0 ** -0.5)))
    coef_ref[...] = gelu * gate_ref[...]


def _peer_v_kernel(idx_hbm, tab_hbm, coef_ref, h1_ref, diag_ref, group_t_ref, y_ref,
                   tab_v, win0, win1, win2, win3, stage_a, stage_b, tab_sem, win_sems, *, tt):

    def compute(stage, g):
        rows8 = pl.ds(pl.multiple_of(g * GROUP, GROUP), GROUP)
        crep = _dot_hi_lo(coef_ref[rows8, :], group_t_ref[...])
        outs = []
        for j in range(GROUP):
            c = jnp.broadcast_to(crep[j:j + 1, :], (ROW_BF16, crep.shape[1])) * diag_ref[...]
            outs.append(_dot_hi_lo(c, _token_rows(stage, j)))
        y_ref[rows8, :] = h1_ref[rows8, :] + jnp.stack(outs).reshape(GROUP, D_MODEL)

    _pipelined_groups(tab_hbm, idx_hbm, tab_v, (win0, win1, win2, win3), (stage_a, stage_b), tab_sem, win_sems,
                      tt, compute)


def _peer_scratch(tt, u_phase):
    shapes = [pltpu.VMEM((PEER_EXPERTS * ROW_WORDS, LANES), I32)]
    shapes += [pltpu.SMEM((GROUP_IDX,), I32)] * NWIN
    shapes += [pltpu.VMEM((GROUP * TOKEN_WORDS, LANES), I32)] * 2
    if u_phase:
        shapes += [pltpu.VMEM((tt, PEER_PAIRS), F32)]
    return shapes + [pltpu.SemaphoreType.DMA, pltpu.SemaphoreType.DMA((NWIN,))]


def _peer_u(idx_flat, tab, hn_packed, gate, tt):
    t = gate.shape[0]
    diag, group, _ = _lane_group_consts()
    any_spec = pl.BlockSpec(memory_space=pl.ANY)
    return pl.pallas_call(
        functools.partial(_peer_u_kernel, tt=tt),
        grid=(t // tt,),
        in_specs=[
            any_spec, any_spec,
            pl.BlockSpec((tt * ROW_WORDS, LANES), lambda i: (i, 0)),
            pl.BlockSpec((tt, PEER_PAIRS), lambda i: (i, 0)),
            pl.BlockSpec(diag.shape, lambda i: (0, 0)),
            pl.BlockSpec(group.shape, lambda i: (0, 0)),
        ],
        out_specs=pl.BlockSpec((tt, PEER_PAIRS), lambda i: (i, 0)),
        out_shape=jax.ShapeDtypeStruct((t, PEER_PAIRS), F32),
        scratch_shapes=_peer_scratch(tt, True),
        compiler_params=pltpu.CompilerParams(vmem_limit_bytes=VMEM_LIMIT),
        name="peer_u",
    )(idx_flat, tab, hn_packed, gate, diag, group)


def _peer_v(idx_flat, tab, coef, h1_rows, tt):
    t = coef.shape[0]
    diag, _, group_t = _lane_group_consts()
    any_spec = pl.BlockSpec(memory_space=pl.ANY)
    return pl.pallas_call(
        functools.partial(_peer_v_kernel, tt=tt),
        grid=(t // tt,),
        in_specs=[
            any_spec, any_spec,
            pl.BlockSpec((tt, PEER_PAIRS), lambda i: (i, 0)),
            pl.BlockSpec((tt, D_MODEL), lambda i: (i, 0)),
            pl.BlockSpec(diag.shape, lambda i: (0, 0)),
            pl.BlockSpec(group_t.shape, lambda i: (0, 0)),
        ],
        out_specs=pl.BlockSpec((tt, D_MODEL), lambda i: (i, 0)),
        out_shape=jax.ShapeDtypeStruct((t, D_MODEL), F32),
        scratch_shapes=_peer_scratch(tt, False),
        compiler_params=pltpu.CompilerParams(vmem_limit_bytes=VMEM_LIMIT),
        name="peer_v",
    )(idx_flat, tab, coef, h1_rows, diag, group_t)


def _rope_tables(pos):
    inv_freq = ROPE_THETA ** (-jnp.arange(ROPE_HALF, dtype=F32) / ROPE_HALF)
    ang = pos.astype(F32)[:, None] * inv_freq[None, :]
    cos, sin = jnp.cos(ang), jnp.sin(ang)
    cos_t = jnp.tile(cos, (1, LANES // ROPE_HALF))
    sin_t = jnp.tile(jnp.concatenate([-sin, sin], axis=1), (1, LANES // MLA_ROPE))
    return cos_t, sin_t


def _pack_in_proj(w_in):
    offs = np.cumsum([0, MLA_Q_RANK, MLA_KV_RANK, MLA_ROPE, GLA_DK, GLA_DK, GLA_DV, GLA_GATE_RANK, GLA_DV,
                      D_MODEL, D_MODEL])
    part = lambda i: w_in[:, int(offs[i]):int(offs[i + 1])]
    zeros = lambda n: jnp.zeros((D_MODEL, n), w_in.dtype)
    cols = [part(0), part(1), part(2), part(6), zeros(LANES - MLA_ROPE - GLA_GATE_RANK), zeros(COL_GQ - COL_KPE - LANES),
            part(3), part(4), part(5), part(7), part(8), part(9)]
    w = jnp.concatenate(cols, axis=1).astype(BF16)
    assert w.shape[1] == PROJ_COLS
    return w


def _head_major(w, widths):
    per = sum(widths)
    w = w.reshape(w.shape[0], -1, per)
    parts, o = [], 0
    for wd in widths:
        parts.append(w[:, :, o:o + wd].reshape(w.shape[0], -1))
        o += wd
    return jnp.concatenate(parts, axis=1)


def _row_tile(m, want):
    t = min(m, want)
    assert m % t == 0
    return t


def kernel(x, meta, norm_mix, w_in, mla_q_norm, mla_w_uq, mla_kv_norm, mla_w_ukv, qn_nope, qn_pe, kn_nope, kn_pe, gla_w_a2, gla_b_a, gla_norm, w_o_mla, w_o_gla, w_out, norm_ffn, peer_w_q, peer_keys, peer_u, peer_v):
    nb, r, d = x.shape
    assert d == D_MODEL and norm_mix.shape[0] == 1 and meta.shape == (N_META, D_MODEL)
    assert r % 256 == 0
    t = nb * r
    x2d = x.reshape(t, d)
    row = lambda a: a.reshape(1, -1).astype(F32)

    w1 = _pack_in_proj(w_in[0])
    g_mix = row(norm_mix[0])
    tile2 = lambda g: jnp.tile(row(g), (1, LANES // MLA_ROPE))
    gkp = jnp.concatenate([row(kn_pe[0]), jnp.ones((1, LANES - MLA_ROPE), F32)], axis=1)
    wa2 = jnp.zeros((LANES, GLA_DK), F32).at[MLA_ROPE:MLA_ROPE + GLA_GATE_RANK].set(gla_w_a2[0]).astype(BF16)
    prep_w = (
        row(mla_q_norm[0]), _head_major(mla_w_uq[0], (MLA_NOPE, MLA_ROPE)).astype(BF16),
        row(mla_kv_norm[0]), _head_major(mla_w_ukv[0], (MLA_NOPE, MLA_V)).astype(BF16),
        row(qn_nope[0]), tile2(qn_pe[0]), row(kn_nope[0]), gkp, wa2, row(gla_b_a[0]),
    )
    g_gla = row(gla_norm[0])

    proj_m = _inproj(meta.astype(F32), g_mix, w1, N_META, PROJ_COLS // 2)
    cos_m, sin_m = _rope_tables(jnp.arange(N_META))
    _, k_m, v_m, la_m = _prep(proj_m, cos_m, sin_m, prep_w, N_META, N_META)
    s_zero = jnp.zeros((GLA_HEADS, GLA_HK, GLA_HV), F32)
    _, s_meta = _gla(proj_m, la_m, s_zero, g_gla, 1, N_META)
    pad = ((0, LANES - N_META), (0, 0))
    k_m, v_m = jnp.pad(k_m, pad), jnp.pad(v_m, pad)

    proj = _inproj(x2d, g_mix, w1, _row_tile(t, 512), PROJ_COLS // 2)
    cos_t, sin_t = _rope_tables(N_META + jnp.arange(r))
    q, k, v, log_a = _prep(proj, cos_t, sin_t, prep_w, 256, r)
    y_a = _attention(q.reshape(nb, r, -1), k.reshape(nb, r, -1), v.reshape(nb, r, -1), k_m, v_m, 256, 512)
    y_b, _ = _gla(proj, log_a, s_meta[0], g_gla, nb, 2 * CHUNK)

    h1, hn, st = _mix(y_a.reshape(t, -1), y_b, proj, x2d, w_o_mla[0].astype(BF16), w_o_gla[0].astype(BF16),
                      w_out[0].astype(BF16), row(norm_ffn[0]), peer_w_q[0].astype(BF16),
                      peer_keys[0].astype(BF16), _row_tile(t, 512))
    idx, gate = _topk(st, 256)
    idx = idx.reshape(-1)

    tt = 128
    coef = _peer_u(idx, _pack_rows(peer_u[0], 512), _pack_rows(hn, 512), gate, tt)
    y = _peer_v(idx, _pack_rows(peer_v[0], 512), coef, h1, tt)
    return y.reshape(nb, r, d)
```

```python
import functools
import math

import jax
import jax.numpy as jnp
import numpy as np
from jax import lax
from jax.experimental import pallas as pl
from jax.experimental.pallas import tpu as pltpu

F32 = jnp.float32
BF16 = jnp.bfloat16
I32 = jnp.int32

D_MODEL = 1024
CHUNK = 64
N_META = 16
EPS = 1e-6

MLA_HEADS = 8
MLA_Q_RANK = 384
MLA_KV_RANK = 256
MLA_NOPE = 128
MLA_ROPE = 64
MLA_V = 128
ROPE_HALF = MLA_ROPE // 2
ROPE_THETA = 10000.0
MLA_QK_PAD = 256
MLA_V_EXT = 256

GLA_HEADS = 4
GLA_DK = D_MODEL // 2
GLA_DV = D_MODEL
GLA_HK = GLA_DK // GLA_HEADS
GLA_HV = GLA_DV // GLA_HEADS
GLA_GATE_RANK = 16
GLA_TAU = 16.0

PEER_HEADS = 8
PEER_NKEYS = 128
PEER_EXPERTS = PEER_NKEYS * PEER_NKEYS
PEER_HALF = 128
PEER_TOPK = 16
PEER_PAIRS = PEER_HEADS * PEER_TOPK

LANES = 128
ROW_WORDS = 4
ROW_BF16 = 8
VMEM_LIMIT = 56 * 1024 * 1024
COL_CQ = 0
COL_CKV = 384
COL_KPE = 640
COL_GQ = 1024
COL_GK = 1536
COL_GV = 2048
COL_GG = 3072
COL_GATE_A = 4096
COL_GATE_B = 5120
PROJ_COLS = 6144


def _dot(a, b):
    return jnp.dot(a, b, preferred_element_type=F32)


def _dot_nt(a, b):
    return lax.dot_general(a, b, (((1,), (1,)), ((), ())), preferred_element_type=F32)


def _dot_tn(a, b):
    return lax.dot_general(a, b, (((0,), (0,)), ((), ())), preferred_element_type=F32)


def _rms(x, g):
    ms = jnp.mean(x * x, axis=-1, keepdims=True)
    return x * lax.rsqrt(ms + EPS) * g


def _split_bf16(x):
    hi = x.astype(BF16)
    lo = (x - hi.astype(F32)).astype(BF16)
    return hi, lo


def _inproj_kernel(x_ref, g_ref, w_ref, o_ref):
    n = _rms(x_ref[...], g_ref[...]).astype(BF16)
    o_ref[...] = _dot(n, w_ref[...]).astype(o_ref.dtype)


def _inproj(x2d, g, w, tm, tn):
    m, d = x2d.shape
    n = w.shape[1]
    return pl.pallas_call(
        _inproj_kernel,
        grid=(n // tn, m // tm),
        in_specs=[
            pl.BlockSpec((tm, d), lambda j, i: (i, 0)),
            pl.BlockSpec((1, d), lambda j, i: (0, 0)),
            pl.BlockSpec((d, tn), lambda j, i: (0, j)),
        ],
        out_specs=pl.BlockSpec((tm, tn), lambda j, i: (i, j)),
        out_shape=jax.ShapeDtypeStruct((m, n), BF16),
        compiler_params=pltpu.CompilerParams(vmem_limit_bytes=VMEM_LIMIT),
        name="inproj",
    )(x2d, g, w)


def _rope_pair_tile(xp, g_tile, cos, sin, lane):
    sq = xp * xp
    lo = lane < MLA_ROPE
    s_lo = jnp.sum(jnp.where(lo, sq, 0.0), axis=-1, keepdims=True)
    s_hi = jnp.sum(jnp.where(lo, 0.0, sq), axis=-1, keepdims=True)
    ms = jnp.where(lo, s_lo, s_hi) * (1.0 / MLA_ROPE)
    y = xp * lax.rsqrt(ms + EPS) * g_tile
    up = pltpu.roll(y, LANES - ROPE_HALF, axis=1)
    dn = pltpu.roll(y, ROPE_HALF, axis=1)
    partner = jnp.where((lane % MLA_ROPE) < ROPE_HALF, up, dn)
    return y * cos + partner * sin


def _prep_kernel(p_ref, cos_ref, sin_ref, qn_ref, wuq_ref, kvn_ref, wukv_ref, gqn_ref, gqp_ref,
                 gkn_ref, gkp_ref, wa2_ref, ba_ref, q_ref, k_ref, v_ref, la_ref):
    tm = p_ref.shape[0]
    lane = lax.broadcasted_iota(I32, (tm, LANES), 1)
    cos = cos_ref[...]
    sin = sin_ref[...]
    scale = (MLA_NOPE + MLA_ROPE) ** -0.5

    cq = p_ref[:, COL_CQ:COL_CQ + MLA_Q_RANK].astype(F32)
    qa = _dot(_rms(cq, qn_ref[...]).astype(BF16), wuq_ref[...])
    ckv = p_ref[:, COL_CKV:COL_CKV + MLA_KV_RANK].astype(F32)
    kva = _dot(_rms(ckv, kvn_ref[...]).astype(BF16), wukv_ref[...])

    misc = p_ref[:, COL_KPE:COL_KPE + LANES]
    kpe = _rope_pair_tile(misc.astype(F32), gkp_ref[...], cos, sin, lane)
    kpe = jnp.where(lane < MLA_ROPE, kpe, 0.0).astype(BF16)

    pe_off = MLA_HEADS * MLA_NOPE
    for j in range(MLA_HEADS // 2):
        t = _rope_pair_tile(qa[:, pe_off + j * LANES: pe_off + (j + 1) * LANES], gqp_ref[...], cos, sin, lane)
        t = t * scale
        even = jnp.where(lane < MLA_ROPE, t, 0.0)
        odd = jnp.where(lane < MLA_ROPE, pltpu.roll(t, MLA_ROPE, axis=1), 0.0)
        for h, pe in ((2 * j, even), (2 * j + 1, odd)):
            qn = _rms(qa[:, h * MLA_NOPE:(h + 1) * MLA_NOPE], gqn_ref[...]) * scale
            q_ref[:, h * MLA_QK_PAD: h * MLA_QK_PAD + MLA_NOPE] = qn.astype(BF16)
            q_ref[:, h * MLA_QK_PAD + MLA_NOPE:(h + 1) * MLA_QK_PAD] = pe.astype(BF16)
    for h in range(MLA_HEADS):
        kn = _rms(kva[:, h * MLA_NOPE:(h + 1) * MLA_NOPE], gkn_ref[...])
        k_ref[:, h * MLA_QK_PAD: h * MLA_QK_PAD + MLA_NOPE] = kn.astype(BF16)
        k_ref[:, h * MLA_QK_PAD + MLA_NOPE:(h + 1) * MLA_QK_PAD] = kpe
    v_off = MLA_HEADS * MLA_NOPE
    for h in range(MLA_HEADS):
        v_ref[:, h * MLA_V_EXT: h * MLA_V_EXT + MLA_V] = kva[:, v_off + h * MLA_V: v_off + (h + 1) * MLA_V].astype(BF16)
        v_ref[:, h * MLA_V_EXT + MLA_V:(h + 1) * MLA_V_EXT] = jnp.ones((tm, MLA_V_EXT - MLA_V), BF16)

    z = _dot(misc, wa2_ref[...]) + ba_ref[...]
    log_sig = jnp.minimum(z, 0.0) - jnp.log(1.0 + jnp.exp(-jnp.abs(z)))
    la_ref[...] = log_sig * (1.0 / GLA_TAU)


def _prep(proj, cos_t, sin_t, wts, tm, rows_per_seq):
    m = proj.shape[0]
    nblk = rows_per_seq // tm
    full = lambda a: pl.BlockSpec(a.shape, lambda i: (0,) * a.ndim)
    return pl.pallas_call(
        _prep_kernel,
        grid=(m // tm,),
        in_specs=[
            pl.BlockSpec((tm, 1024), lambda i: (i, 0)),
            pl.BlockSpec((tm, LANES), lambda i: (i % nblk, 0)),
            pl.BlockSpec((tm, LANES), lambda i: (i % nblk, 0)),
        ] + [full(w) for w in wts],
        out_specs=[
            pl.BlockSpec((tm, MLA_HEADS * MLA_QK_PAD), lambda i: (i, 0)),
            pl.BlockSpec((tm, MLA_HEADS * MLA_QK_PAD), lambda i: (i, 0)),
            pl.BlockSpec((tm, MLA_HEADS * MLA_V_EXT), lambda i: (i, 0)),
            pl.BlockSpec((tm, GLA_DK), lambda i: (i, 0)),
        ],
        out_shape=[
            jax.ShapeDtypeStruct((m, MLA_HEADS * MLA_QK_PAD), BF16),
            jax.ShapeDtypeStruct((m, MLA_HEADS * MLA_QK_PAD), BF16),
            jax.ShapeDtypeStruct((m, MLA_HEADS * MLA_V_EXT), BF16),
            jax.ShapeDtypeStruct((m, GLA_DK), F32),
        ],
        compiler_params=pltpu.CompilerParams(vmem_limit_bytes=VMEM_LIMIT),
        name="mla_gla_prep",
    )(proj, cos_t, sin_t, *wts)


def _attn_kernel(q_ref, k_ref, v_ref, km_ref, vm_ref, o_ref, *, tq, tk):
    i = pl.program_id(1)
    qs = [slice(h * MLA_QK_PAD, (h + 1) * MLA_QK_PAD) for h in range(MLA_HEADS)]
    vs = [slice(h * MLA_V_EXT, (h + 1) * MLA_V_EXT) for h in range(MLA_HEADS)]

    def update(h, state, kblk, vblk, mask):
        m, acc = state
        s = _dot_nt(q_ref[0, :, qs[h]], kblk)
        if mask is not None:
            s = jnp.where(mask, s, -jnp.inf)
        m_new = jnp.maximum(m, jnp.max(s, axis=-1, keepdims=True))
        p = jnp.exp(s - m_new).astype(BF16)
        return m_new, jnp.exp(m - m_new) * acc + _dot(p, vblk)

    meta_mask = lax.broadcasted_iota(I32, (tq, LANES), 1) < N_META
    init = (jnp.full((tq, 1), -jnp.inf, F32), jnp.zeros((tq, MLA_V_EXT), F32))
    states = tuple(update(h, init, km_ref[:, qs[h]], vm_ref[:, vs[h]], meta_mask) for h in range(MLA_HEADS))

    def body(j, states):
        r0 = pl.multiple_of(j * tk, tk)
        return tuple(update(h, states[h], k_ref[0, pl.ds(r0, tk), qs[h]], v_ref[0, pl.ds(r0, tk), vs[h]], None)
                     for h in range(MLA_HEADS))

    nfull = (i * tq) // tk
    states = lax.fori_loop(0, nfull, body, states)
    r0 = pl.multiple_of(nfull * tk, tk)
    q_chunk = (i * tq + lax.broadcasted_iota(I32, (tq, tk), 0)) // CHUNK
    k_chunk = (r0 + lax.broadcasted_iota(I32, (tq, tk), 1)) // CHUNK
    last_mask = k_chunk <= q_chunk
    for h in range(MLA_HEADS):
        _, acc = update(h, states[h], k_ref[0, pl.ds(r0, tk), qs[h]], v_ref[0, pl.ds(r0, tk), vs[h]], last_mask)
        o_ref[0, :, h * MLA_V:(h + 1) * MLA_V] = (acc[:, :MLA_V] / acc[:, MLA_V:]).astype(o_ref.dtype)


def _attention(q, k, v, k_meta, v_meta, tq, tk):
    b, r, _ = q.shape
    assert tk % tq == 0 and r % tk == 0
    return pl.pallas_call(
        functools.partial(_attn_kernel, tq=tq, tk=tk),
        grid=(b, r // tq),
        in_specs=[
            pl.BlockSpec((1, tq, q.shape[2]), lambda bi, i: (bi, i, 0)),
            pl.BlockSpec((1, r, k.shape[2]), lambda bi, i: (bi, 0, 0)),
            pl.BlockSpec((1, r, v.shape[2]), lambda bi, i: (bi, 0, 0)),
            pl.BlockSpec(k_meta.shape, lambda bi, i: (0, 0)),
            pl.BlockSpec(v_meta.shape, lambda bi, i: (0, 0)),
        ],
        out_specs=pl.BlockSpec((1, tq, MLA_HEADS * MLA_V), lambda bi, i: (bi, i, 0)),
        out_shape=jax.ShapeDtypeStruct((b, r, MLA_HEADS * MLA_V), BF16),
        compiler_params=pltpu.CompilerParams(vmem_limit_bytes=VMEM_LIMIT),
        name="mla_attention",
    )(q, k, v, k_meta, v_meta)


def _gla_level_consts(c):
    idx = np.arange(c)
    tril = (idx[None, :] <= idx[:, None]).astype(np.float32)
    sels, masks = [], []
    m = c // 2
    while m >= 1:
        ref = (idx // (2 * m)) * (2 * m) + m - 1
        sels.append(tril[ref])
        same = (idx[:, None] // (2 * m)) == (idx[None, :] // (2 * m))
        masks.append(same & ((idx[:, None] % (2 * m)) >= m) & ((idx[None, :] % (2 * m)) < m))
        m //= 2
    return tril, np.stack(sels), np.stack(masks).astype(np.float32)


def _gla_kernel(gq_ref, gk_ref, gv_ref, gg_ref, la_ref, s0_ref, gn_ref, tril_ref, sel_ref, msk_ref,
                y_ref, sout_ref, s_scr, *, c, nlev):
    ci = pl.program_id(1)

    @pl.when(ci == 0)
    def _():
        s_scr[...] = s0_ref[...]

    la = la_ref[...]
    hi = la.astype(BF16)
    rest = la - hi.astype(F32)
    mid = rest.astype(BF16)
    lo = (rest - mid.astype(F32)).astype(BF16)
    csel = jnp.concatenate([tril_ref[...]] + [sel_ref[lv] for lv in range(nlev)], axis=0).astype(BF16)
    cum = _dot(csel, hi) + _dot(csel, mid) + _dot(csel, lo)
    b_all = cum[:c]
    brefs = [cum[(lv + 1) * c:(lv + 2) * c] for lv in range(nlev)]
    eye = (lax.broadcasted_iota(I32, (c, c), 0) == lax.broadcasted_iota(I32, (c, c), 1))
    eye_k = (lax.broadcasted_iota(I32, (GLA_HK, GLA_HK), 0) == lax.broadcasted_iota(I32, (GLA_HK, GLA_HK), 1))

    for h in range(GLA_HEADS):
        ks = slice(h * GLA_HK, (h + 1) * GLA_HK)
        vs = slice(h * GLA_HV, (h + 1) * GLA_HV)
        q = gq_ref[:, ks].astype(F32) * (GLA_HK ** -0.5)
        k = gk_ref[:, ks].astype(F32)
        v = gv_ref[:, vs]
        bh = b_all[:, ks]
        s_old = s_scr[h]

        o = _dot((q * jnp.exp(bh)).astype(BF16), s_old.astype(BF16))
        a = jnp.where(eye, _dot_nt(q.astype(BF16), k.astype(BF16)), 0.0)
        for lv in range(nlev):
            br = brefs[lv][:, ks]
            qd = q * jnp.exp(jnp.minimum(bh - br, 0.0))
            kd = k * jnp.exp(jnp.minimum(br - bh, 0.0))
            a = a + msk_ref[lv] * _dot_nt(qd.astype(BF16), kd.astype(BF16))
        o = o + _dot(a.astype(BF16), v)

        b_last = bh[c - 1:c, :]
        kdec = k * jnp.exp(b_last - bh)
        dec_col = jnp.sum(jnp.where(eye_k, jnp.exp(b_last), 0.0), axis=1, keepdims=True)
        s_scr[h] = dec_col * s_old + _dot_tn(kdec.astype(BF16), v)

        gg = gg_ref[:, vs].astype(F32)
        silu = gg / (1.0 + jnp.exp(-gg))
        y_ref[:, vs] = (_rms(o, gn_ref[...]) * silu).astype(y_ref.dtype)

    @pl.when(ci == pl.num_programs(1) - 1)
    def _():
        sout_ref[0] = s_scr[...]


def _gla(proj, log_a, s0, g_norm, nb, c):
    m = proj.shape[0]
    nc = m // nb // c
    tril, sels, masks = _gla_level_consts(c)
    nlev = sels.shape[0]
    row = lambda bi, ci: bi * nc + ci
    return pl.pallas_call(
        functools.partial(_gla_kernel, c=c, nlev=nlev),
        grid=(nb, nc),
        in_specs=[
            pl.BlockSpec((c, GLA_DK), lambda bi, ci: (row(bi, ci), COL_GQ // GLA_DK)),
            pl.BlockSpec((c, GLA_DK), lambda bi, ci: (row(bi, ci), COL_GK // GLA_DK)),
            pl.BlockSpec((c, GLA_DV), lambda bi, ci: (row(bi, ci), COL_GV // GLA_DV)),
            pl.BlockSpec((c, GLA_DV), lambda bi, ci: (row(bi, ci), COL_GG // GLA_DV)),
            pl.BlockSpec((c, GLA_DK), lambda bi, ci: (row(bi, ci), 0)),
            pl.BlockSpec(s0.shape, lambda bi, ci: (0, 0, 0)),
            pl.BlockSpec(g_norm.shape, lambda bi, ci: (0, 0)),
            pl.BlockSpec(tril.shape, lambda bi, ci: (0, 0)),
            pl.BlockSpec(sels.shape, lambda bi, ci: (0, 0, 0)),
            pl.BlockSpec(masks.shape, lambda bi, ci: (0, 0, 0)),
        ],
        out_specs=[
            pl.BlockSpec((c, GLA_DV), lambda bi, ci: (row(bi, ci), 0)),
            pl.BlockSpec((1,) + s0.shape, lambda bi, ci: (bi, 0, 0, 0)),
        ],
        out_shape=[
            jax.ShapeDtypeStruct((m, GLA_DV), BF16),
            jax.ShapeDtypeStruct((nb,) + s0.shape, F32),
        ],
        scratch_shapes=[pltpu.VMEM(s0.shape, F32)],
        compiler_params=pltpu.CompilerParams(vmem_limit_bytes=VMEM_LIMIT),
        name="gla",
    )(proj, proj, proj, proj, log_a, s0, g_norm, jnp.asarray(tril), jnp.asarray(sels), jnp.asarray(masks))


def _mix_kernel(ya_ref, yb_ref, ga_ref, gb_ref, x_ref, wa_ref, wb_ref, wo_ref, gf_ref, wq_ref, keys_ref,
                h1_ref, hn_ref, st_ref):
    sig = lambda t: 1.0 / (1.0 + jnp.exp(-t))
    a = _dot(ya_ref[...], wa_ref[...])
    b = _dot(yb_ref[...], wb_ref[...])
    mix = sig(ga_ref[...].astype(F32)) * a + sig(gb_ref[...].astype(F32)) * b
    h1 = x_ref[...] + _dot(mix.astype(BF16), wo_ref[...])
    h1_ref[...] = h1
    hn = _rms(h1, gf_ref[...]).astype(BF16)
    hn_ref[...] = hn
    qp = _dot(hn, wq_ref[...]).astype(BF16)
    for hp in range(2 * PEER_HEADS):
        st_ref[hp] = _dot_nt(keys_ref[hp % 2], qp[:, hp * PEER_HALF:(hp + 1) * PEER_HALF])


def _mix(y_a, y_b, proj, x2d, w_a, w_b, w_o, g_ffn, w_q, keys, tm):
    m = x2d.shape[0]
    full = lambda a: pl.BlockSpec(a.shape, lambda i: (0,) * a.ndim)
    return pl.pallas_call(
        _mix_kernel,
        grid=(m // tm,),
        in_specs=[
            pl.BlockSpec((tm, D_MODEL), lambda i: (i, 0)),
            pl.BlockSpec((tm, D_MODEL), lambda i: (i, 0)),
            pl.BlockSpec((tm, D_MODEL), lambda i: (i, COL_GATE_A // D_MODEL)),
            pl.BlockSpec((tm, D_MODEL), lambda i: (i, COL_GATE_B // D_MODEL)),
            pl.BlockSpec((tm, D_MODEL), lambda i: (i, 0)),
            full(w_a), full(w_b), full(w_o), full(g_ffn), full(w_q), full(keys),
        ],
        out_specs=[
            pl.BlockSpec((tm, D_MODEL), lambda i: (i, 0)),
            pl.BlockSpec((tm, D_MODEL), lambda i: (i, 0)),
            pl.BlockSpec((2 * PEER_HEADS, PEER_NKEYS, tm), lambda i: (0, 0, i)),
        ],
        out_shape=[
            jax.ShapeDtypeStruct((m, D_MODEL), F32),
            jax.ShapeDtypeStruct((m, D_MODEL), BF16),
            jax.ShapeDtypeStruct((2 * PEER_HEADS, PEER_NKEYS, m), F32),
        ],
        compiler_params=pltpu.CompilerParams(vmem_limit_bytes=VMEM_LIMIT),
        name="mix_out_peer_scores",
    )(y_a, y_b, proj, proj, x2d, w_a, w_b, w_o, g_ffn, w_q, keys)


def _topk_rows(s, k, n):
    half = n // 2
    a, b = s[:half], s[half:]
    ra = lax.broadcasted_iota(I32, a.shape, 0).astype(F32)
    rb = ra + float(half)
    a_wins = a >= b
    w, l = jnp.where(a_wins, a, b), jnp.where(a_wins, b, a)
    wi, li = jnp.where(a_wins, ra, rb), jnp.where(a_wins, rb, ra)
    vals, ids = [], []
    for _ in range(k):
        m = jnp.max(w, axis=0, keepdims=True)
        am = jnp.min(jnp.where(w == m, wi, float(n)), axis=0, keepdims=True)
        vals.append(m)
        ids.append(am)
        hit = wi == am
        w, wi, l = jnp.where(hit, l, w), jnp.where(hit, li, wi), jnp.where(hit, -jnp.inf, l)
    return jnp.concatenate(vals, axis=0), jnp.concatenate(ids, axis=0)


def _topk_head(st_ref, h, idx_ref, gate_ref):
    state = None
    for part in range(TOPK_PARTS):
        state = _topk_head_part(st_ref, h, idx_ref, gate_ref, part, state)


TOPK_PARTS = 4


def _topk_head_part(st_ref, h, idx_ref, gate_ref, part, state):
    if part == 0:
        return _topk_rows(st_ref[2 * h], PEER_TOPK, PEER_NKEYS)
    if part == 1:
        return state + _topk_rows(st_ref[2 * h + 1], PEER_TOPK, PEER_NKEYS)
    if part == 2:
        cs, ci, pos = _topk_candidates(*state)
        return _topk_extract(cs, ci, pos, [], [], PEER_TOPK // 2)
    cs, ci, pos, vals, ids = _topk_extract(*state, PEER_TOPK // 2)
    best = jnp.concatenate(vals, axis=0)
    e = jnp.exp(best - best[0:1])
    gate_ref[h] = e / jnp.sum(e, axis=0, keepdims=True)
    idx_ref[h] = jnp.concatenate(ids, axis=0).astype(I32)
    return None


def _topk_extract(cs, ci, pos, vals, ids, steps):
    n = float(PEER_TOPK * PEER_TOPK)
    vals, ids = list(vals), list(ids)
    for _ in range(steps):
        m = jnp.max(cs, axis=0, keepdims=True)
        ap = jnp.min(jnp.where(cs == m, pos, n), axis=0, keepdims=True)
        sel = pos == ap
        vals.append(m)
        ids.append(jnp.sum(jnp.where(sel, ci, 0.0), axis=0, keepdims=True))
        cs = jnp.where(sel, -jnp.inf, cs)
    return cs, ci, pos, vals, ids


def _topk_candidates(v1, i1, v2, i2):
    half = PEER_TOPK // 2
    sel = [(slice(0, 1), slice(None))] + [(slice(a, a + 1), slice(0, half)) for a in range(1, half)]
    sel.append((slice(half, None), slice(0, 1)))
    cs = jnp.concatenate([v1[sa] + v2[sb] for sa, sb in sel], axis=0)
    ci = jnp.concatenate([i1[sa] * PEER_NKEYS + i2[sb] for sa, sb in sel], axis=0)
    r = lax.broadcasted_iota(I32, cs.shape, 0)
    mid = PEER_TOPK + (r - PEER_TOPK) // half * PEER_TOPK + (r - PEER_TOPK) % half
    tail = (r - (PEER_TOPK + (half - 1) * half) + half) * PEER_TOPK
    pos = jnp.where(r < PEER_TOPK, r, jnp.where(r < PEER_TOPK + (half - 1) * half, mid, tail))
    return cs, ci, pos.astype(F32)


def _pack_kernel(x_ref, o_ref):
    rows = x_ref.shape[0]
    x = x_ref[...].astype(F32).reshape(rows * ROW_BF16, LANES)
    o_ref[...] = pltpu.bitcast(x.astype(BF16), I32)


def _pack_rows(a, rows):
    n = a.shape[0]
    return pl.pallas_call(
        _pack_kernel,
        grid=(n // rows,),
        in_specs=[pl.BlockSpec((rows, a.shape[1]), lambda i: (i, 0))],
        out_specs=pl.BlockSpec((rows * ROW_WORDS, LANES), lambda i: (i, 0)),
        out_shape=jax.ShapeDtypeStruct((n * ROW_WORDS, LANES), I32),
        compiler_params=pltpu.CompilerParams(vmem_limit_bytes=VMEM_LIMIT),
        name="pack_rows",
    )(a)


def _lane_group_consts():
    lane = np.arange(PEER_PAIRS * ROW_BF16)
    diag = (lane[None, :] % ROW_BF16) == np.arange(ROW_BF16)[:, None]
    group = (lane[:, None] // ROW_BF16) == np.arange(PEER_PAIRS)[None, :]
    return jnp.asarray(diag, F32), jnp.asarray(group, BF16), jnp.asarray(group.T, BF16)


GROUP = 8
TOKEN_WORDS = PEER_PAIRS * ROW_WORDS
GROUP_IDX = GROUP * PEER_PAIRS
NWIN = 4


def _gather_group(tab_v, win, stage):
    for j in range(GROUP_IDX):
        e = pl.multiple_of(win[j // PEER_PAIRS, j % PEER_PAIRS], ROW_WORDS)
        stage[j * ROW_WORDS:(j + 1) * ROW_WORDS, :] = tab_v[pl.ds(e, ROW_WORDS), :]


def _token_rows(stage, j):
    return pltpu.bitcast(stage[j * TOKEN_WORDS:(j + 1) * TOKEN_WORDS, :], BF16)


def _dot_hi_lo(x, w):
    n = x.shape[0]
    hi = x.astype(BF16).astype(F32)
    r = _dot(jnp.concatenate([hi, x - hi], axis=0).astype(BF16), w)
    return r[:n] + r[n:]


def _pipelined_groups(tab_hbm, tab_v, wins, stages, tab_sem, win_sems, tt, idx_rows, compute, side=None):
    i = pl.program_id(0)
    nsteps = pl.num_programs(0)
    ng = tt // GROUP
    assert ng % NWIN == 0 and ng >= 2 * NWIN and NWIN % len(stages) == 0

    def dma(w, g):
        return pltpu.make_async_copy(idx_rows(g), wins[w % NWIN], win_sems.at[w % NWIN])

    @pl.when(i == 0)
    def _():
        table = pltpu.make_async_copy(tab_hbm, tab_v, tab_sem)
        table.start()
        for w in range(NWIN):
            dma(w, w).start()
        table.wait()

    def gather_next_and_compute(g, u):
        dma(u + 1, g + 1).wait()
        dma(u, g + NWIN).start()
        _gather_group(tab_v, wins[(u + 1) % NWIN], stages[(u + 1) % len(stages)])
        compute(stages[u % len(stages)], g)

    dma(0, 0).wait()
    _gather_group(tab_v, wins[0], stages[0])
    if side:
        side[0]()

    def quad(k, carry):
        state = None
        for u in range(NWIN):
            gather_next_and_compute(NWIN * k + u, u)
            if side:
                state = side[1](k, u, state)
        return carry

    lax.fori_loop(0, ng // NWIN - 1, quad, 0)
    if side:
        side[2]()
    for u in range(NWIN - 1):
        gather_next_and_compute(ng - NWIN + u, u)
    compute(stages[(NWIN - 1) % len(stages)], ng - 1)
    dma(NWIN - 1, ng - 1 + NWIN).start()

    @pl.when(i == nsteps - 1)
    def _():
        for w in range(NWIN):
            dma(w, 0).wait()


def _peer_u_kernel(tab_hbm, hp_ref, st0_ref, stn_ref, diag_ref, group_ref, coef_ref, idx_out_ref,
                   tab_v, win0, win1, win2, win3, stage_a, stage_b, act, idx_v, gate_v, tk_idx, tk_gate,
                   tab_sem, win_sems, *, tt):
    i = pl.program_id(0)
    ng = tt // GROUP
    cur, nxt = i % 2, (i + 1) % 2
    assert ng // NWIN == PEER_HEADS

    def finish_topk(slot):
        idx_v[slot] = tk_idx[...].reshape(PEER_PAIRS, tt).T * ROW_WORDS
        gate_v[slot] = tk_gate[...].reshape(PEER_PAIRS, tt).T

    @pl.when(i == 0)
    def _():
        def head(h, carry):
            _topk_head(st0_ref, h, tk_idx, tk_gate)
            return carry
        lax.fori_loop(0, PEER_HEADS, head, 0)
        finish_topk(0)

    def idx_rows(g):
        if isinstance(g, int):
            slot, g = (nxt, g - ng) if g >= ng else (cur, g)
            return idx_v.at[slot, pl.ds(g * GROUP, GROUP), :]
        return idx_v.at[cur, pl.ds(pl.multiple_of(g * GROUP, GROUP), GROUP), :]

    def compute(stage, g):
        zs = []
        for j in range(GROUP):
            r0 = pl.multiple_of((g * GROUP + j) * ROW_WORDS, ROW_WORDS)
            ht = pltpu.bitcast(hp_ref[pl.ds(r0, ROW_WORDS), :], BF16)
            zs.append(_dot_nt(ht, _token_rows(stage, j)) * diag_ref[...])
        a = _dot_hi_lo(jnp.concatenate(zs, axis=0), group_ref[...])
        acts = [jnp.sum(a[j * ROW_BF16:(j + 1) * ROW_BF16], axis=0, keepdims=True) for j in range(GROUP)]
        rows = pl.ds(pl.multiple_of(g * GROUP, GROUP), GROUP)
        act[rows, :] = jnp.concatenate(acts, axis=0)

    side = (
        lambda: _topk_head(stn_ref, PEER_HEADS - 1, tk_idx, tk_gate),
        lambda k, u, state: _topk_head_part(stn_ref, k, tk_idx, tk_gate, u, state),
        lambda: finish_topk(nxt),
    )
    _pipelined_groups(tab_hbm, tab_v, (win0, win1, win2, win3), (stage_a, stage_b), tab_sem, win_sems, tt,
                      idx_rows, compute, side)
    idx_out_ref[...] = idx_v[cur]
    a = act[...]
    gelu = 0.5 * a * (1.0 + lax.erf(a * (2.0 ** -0.5)))
    coef_ref[...] = gelu * gate_v[cur]


def _peer_v_kernel(idx_hbm, tab_hbm, coef_ref, h1_ref, diag_ref, group_t_ref, y_ref,
                   tab_v, win0, win1, win2, win3, stage_a, stage_b, tab_sem, win_sems, *, tt):
    i = pl.program_id(0)
    last_group = pl.num_programs(0) * (tt // GROUP) - 1

    def idx_rows(g):
        group = jnp.minimum(i * (tt // GROUP) + g, last_group)
        return idx_hbm.at[pl.ds(pl.multiple_of(group * GROUP, GROUP), GROUP), :]

    def compute(stage, g):
        rows8 = pl.ds(pl.multiple_of(g * GROUP, GROUP), GROUP)
        crep = _dot_hi_lo(coef_ref[rows8, :], group_t_ref[...])
        outs = []
        for j in range(GROUP):
            c = jnp.broadcast_to(crep[j:j + 1, :], (ROW_BF16, crep.shape[1])) * diag_ref[...]
            outs.append(_dot_hi_lo(c, _token_rows(stage, j)))
        y_ref[rows8, :] = h1_ref[rows8, :] + jnp.stack(outs).reshape(GROUP, D_MODEL)

    _pipelined_groups(tab_hbm, tab_v, (win0, win1, win2, win3), (stage_a, stage_b), tab_sem, win_sems, tt,
                      idx_rows, compute)


def _peer_scratch(tt, u_phase):
    shapes = [pltpu.VMEM((PEER_EXPERTS * ROW_WORDS, LANES), I32)]
    shapes += [pltpu.SMEM((GROUP, PEER_PAIRS), I32)] * NWIN
    shapes += [pltpu.VMEM((GROUP * TOKEN_WORDS, LANES), I32)] * 2
    if u_phase:
        shapes += [
            pltpu.VMEM((tt, PEER_PAIRS), F32),
            pltpu.VMEM((2, tt, PEER_PAIRS), I32),
            pltpu.VMEM((2, tt, PEER_PAIRS), F32),
            pltpu.VMEM((PEER_HEADS, PEER_TOPK, tt), I32),
            pltpu.VMEM((PEER_HEADS, PEER_TOPK, tt), F32),
        ]
    return shapes + [pltpu.SemaphoreType.DMA, pltpu.SemaphoreType.DMA((NWIN,))]


def _peer_u(tab, hn_packed, st, tt):
    t = st.shape[2]
    n = t // tt
    diag, group, _ = _lane_group_consts()
    st_block = (2 * PEER_HEADS, PEER_NKEYS, tt)
    return pl.pallas_call(
        functools.partial(_peer_u_kernel, tt=tt),
        grid=(n,),
        in_specs=[
            pl.BlockSpec(memory_space=pl.ANY),
            pl.BlockSpec((tt * ROW_WORDS, LANES), lambda i: (i, 0)),
            pl.BlockSpec(st_block, lambda i: (0, 0, 0)),
            pl.BlockSpec(st_block, lambda i: (0, 0, jnp.minimum(i + 1, n - 1))),
            pl.BlockSpec(diag.shape, lambda i: (0, 0)),
            pl.BlockSpec(group.shape, lambda i: (0, 0)),
        ],
        out_specs=[
            pl.BlockSpec((tt, PEER_PAIRS), lambda i: (i, 0)),
            pl.BlockSpec((tt, PEER_PAIRS), lambda i: (i, 0)),
        ],
        out_shape=[
            jax.ShapeDtypeStruct((t, PEER_PAIRS), F32),
            jax.ShapeDtypeStruct((t, PEER_PAIRS), I32),
        ],
        scratch_shapes=_peer_scratch(tt, True),
        compiler_params=pltpu.CompilerParams(vmem_limit_bytes=VMEM_LIMIT),
        name="peer_u",
    )(tab, hn_packed, st, st, diag, group)


def _peer_v(idx_flat, tab, coef, h1_rows, tt):
    t = coef.shape[0]
    diag, _, group_t = _lane_group_consts()
    any_spec = pl.BlockSpec(memory_space=pl.ANY)
    return pl.pallas_call(
        functools.partial(_peer_v_kernel, tt=tt),
        grid=(t // tt,),
        in_specs=[
            any_spec, any_spec,
            pl.BlockSpec((tt, PEER_PAIRS), lambda i: (i, 0)),
            pl.BlockSpec((tt, D_MODEL), lambda i: (i, 0)),
            pl.BlockSpec(diag.shape, lambda i: (0, 0)),
            pl.BlockSpec(group_t.shape, lambda i: (0, 0)),
        ],
        out_specs=pl.BlockSpec((tt, D_MODEL), lambda i: (i, 0)),
        out_shape=jax.ShapeDtypeStruct((t, D_MODEL), F32),
        scratch_shapes=_peer_scratch(tt, False),
        compiler_params=pltpu.CompilerParams(vmem_limit_bytes=VMEM_LIMIT),
        name="peer_v",
    )(idx_flat, tab, coef, h1_rows, diag, group_t)


def _rope_tables(pos):
    inv_freq = ROPE_THETA ** (-jnp.arange(ROPE_HALF, dtype=F32) / ROPE_HALF)
    ang = pos.astype(F32)[:, None] * inv_freq[None, :]
    cos, sin = jnp.cos(ang), jnp.sin(ang)
    cos_t = jnp.tile(cos, (1, LANES // ROPE_HALF))
    sin_t = jnp.tile(jnp.concatenate([-sin, sin], axis=1), (1, LANES // MLA_ROPE))
    return cos_t, sin_t


def _pack_in_proj(w_in):
    offs = np.cumsum([0, MLA_Q_RANK, MLA_KV_RANK, MLA_ROPE, GLA_DK, GLA_DK, GLA_DV, GLA_GATE_RANK, GLA_DV,
                      D_MODEL, D_MODEL])
    part = lambda i: w_in[:, int(offs[i]):int(offs[i + 1])]
    zeros = lambda n: jnp.zeros((D_MODEL, n), w_in.dtype)
    cols = [part(0), part(1), part(2), part(6), zeros(LANES - MLA_ROPE - GLA_GATE_RANK), zeros(COL_GQ - COL_KPE - LANES),
            part(3), part(4), part(5), part(7), part(8), part(9)]
    w = jnp.concatenate(cols, axis=1).astype(BF16)
    assert w.shape[1] == PROJ_COLS
    return w


def _head_major(w, widths):
    per = sum(widths)
    w = w.reshape(w.shape[0], -1, per)
    parts, o = [], 0
    for wd in widths:
        parts.append(w[:, :, o:o + wd].reshape(w.shape[0], -1))
        o += wd
    return jnp.concatenate(parts, axis=1)


def _row_tile(m, want):
    t = min(m, want)
    assert m % t == 0
    return t


def kernel(x, meta, norm_mix, w_in, mla_q_norm, mla_w_uq, mla_kv_norm, mla_w_ukv, qn_nope, qn_pe, kn_nope, kn_pe, gla_w_a2, gla_b_a, gla_norm, w_o_mla, w_o_gla, w_out, norm_ffn, peer_w_q, peer_keys, peer_u, peer_v):
    nb, r, d = x.shape
    assert d == D_MODEL and norm_mix.shape[0] == 1 and meta.shape == (N_META, D_MODEL)
    assert r % 256 == 0
    t = nb * r
    x2d = x.reshape(t, d)
    row = lambda a: a.reshape(1, -1).astype(F32)

    w1 = _pack_in_proj(w_in[0])
    g_mix = row(norm_mix[0])
    tile2 = lambda g: jnp.tile(row(g), (1, LANES // MLA_ROPE))
    gkp = jnp.concatenate([row(kn_pe[0]), jnp.ones((1, LANES - MLA_ROPE), F32)], axis=1)
    wa2 = jnp.zeros((LANES, GLA_DK), F32).at[MLA_ROPE:MLA_ROPE + GLA_GATE_RANK].set(gla_w_a2[0]).astype(BF16)
    prep_w = (
        row(mla_q_norm[0]), _head_major(mla_w_uq[0], (MLA_NOPE, MLA_ROPE)).astype(BF16),
        row(mla_kv_norm[0]), _head_major(mla_w_ukv[0], (MLA_NOPE, MLA_V)).astype(BF16),
        row(qn_nope[0]), tile2(qn_pe[0]), row(kn_nope[0]), gkp, wa2, row(gla_b_a[0]),
    )
    g_gla = row(gla_norm[0])

    proj_m = _inproj(meta.astype(F32), g_mix, w1, N_META, PROJ_COLS // 2)
    cos_m, sin_m = _rope_tables(jnp.arange(N_META))
    _, k_m, v_m, la_m = _prep(proj_m, cos_m, sin_m, prep_w, N_META, N_META)
    s_zero = jnp.zeros((GLA_HEADS, GLA_HK, GLA_HV), F32)
    _, s_meta = _gla(proj_m, la_m, s_zero, g_gla, 1, N_META)
    pad = ((0, LANES - N_META), (0, 0))
    k_m, v_m = jnp.pad(k_m, pad), jnp.pad(v_m, pad)

    proj = _inproj(x2d, g_mix, w1, _row_tile(t, 512), PROJ_COLS // 2)
    cos_t, sin_t = _rope_tables(N_META + jnp.arange(r))
    q, k, v, log_a = _prep(proj, cos_t, sin_t, prep_w, 256, r)
    y_a = _attention(q.reshape(nb, r, -1), k.reshape(nb, r, -1), v.reshape(nb, r, -1), k_m, v_m, 256, 512)
    y_b, _ = _gla(proj, log_a, s_meta[0], g_gla, nb, 2 * CHUNK)

    h1, hn, st = _mix(y_a.reshape(t, -1), y_b, proj, x2d, w_o_mla[0].astype(BF16), w_o_gla[0].astype(BF16),
                      w_out[0].astype(BF16), row(norm_ffn[0]), peer_w_q[0].astype(BF16),
                      peer_keys[0].astype(BF16), _row_tile(t, 512))
    coef, idx = _peer_u(_pack_rows(peer_u[0], 512), _pack_rows(hn, 512), st, 256)
    y = _peer_v(idx, _pack_rows(peer_v[0], 512), coef, h1, 128)
    return y.reshape(nb, r, d)
```

```python
import functools
from typing import NamedTuple

import jax
import jax.numpy as jnp
import numpy as np
from jax import lax
from jax.experimental import pallas as pl
from jax.experimental.pallas import tpu as pltpu

F32 = jnp.float32
BF16 = jnp.bfloat16
I32 = jnp.int32

D_MODEL = 1024
CHUNK = 64
N_META = 16
EPS = 1e-6

MLA_HEADS = 8
MLA_Q_RANK = 384
MLA_KV_RANK = 256
MLA_NOPE = 128
MLA_ROPE = 64
MLA_V = 128
ROPE_HALF = MLA_ROPE // 2
ROPE_THETA = 10000.0
MLA_QK_PAD = 256
MLA_V_EXT = 256

GLA_HEADS = 4
GLA_DK = D_MODEL // 2
GLA_DV = D_MODEL
GLA_HK = GLA_DK // GLA_HEADS
GLA_HV = GLA_DV // GLA_HEADS
GLA_GATE_RANK = 16
GLA_TAU = 16.0

PEER_HEADS = 8
PEER_NKEYS = 128
PEER_EXPERTS = PEER_NKEYS * PEER_NKEYS
PEER_HALF = 128
PEER_TOPK = 16
PEER_PAIRS = PEER_HEADS * PEER_TOPK

LANES = 128
ROW_WORDS = 4
ROW_BF16 = 8
VMEM_LIMIT = 56 * 1024 * 1024
COL_CQ = 0
COL_CKV = 384
COL_KPE = 640
COL_GQ = 1024
COL_GK = 1536
COL_GV = 2048
COL_GG = 3072
COL_GATE_A = 4096
COL_GATE_B = 5120
PROJ_COLS = 6144


def _dot(a, b):
    return jnp.dot(a, b, preferred_element_type=F32)


def _dot_nt(a, b):
    return lax.dot_general(a, b, (((1,), (1,)), ((), ())), preferred_element_type=F32)


def _dot_tn(a, b):
    return lax.dot_general(a, b, (((0,), (0,)), ((), ())), preferred_element_type=F32)


def _rms(x, g):
    ms = jnp.mean(x * x, axis=-1, keepdims=True)
    return x * lax.rsqrt(ms + EPS) * g


def _inproj_kernel(x_ref, g_ref, w_ref, o_ref):
    n = _rms(x_ref[...], g_ref[...]).astype(BF16)
    o_ref[...] = _dot(n, w_ref[...]).astype(o_ref.dtype)


def _inproj(x2d, g, w, tm, tn):
    m, d = x2d.shape
    n = w.shape[1]
    return pl.pallas_call(
        _inproj_kernel,
        grid=(n // tn, m // tm),
        in_specs=[
            pl.BlockSpec((tm, d), lambda j, i: (i, 0)),
            pl.BlockSpec((1, d), lambda j, i: (0, 0)),
            pl.BlockSpec((d, tn), lambda j, i: (0, j)),
        ],
        out_specs=pl.BlockSpec((tm, tn), lambda j, i: (i, j)),
        out_shape=jax.ShapeDtypeStruct((m, n), BF16),
        compiler_params=pltpu.CompilerParams(vmem_limit_bytes=VMEM_LIMIT),
        name="inproj",
    )(x2d, g, w)


def _rope_pair_tile(xp, g_tile, cos, sin, lane):
    sq = xp * xp
    lo = lane < MLA_ROPE
    s_lo = jnp.sum(jnp.where(lo, sq, 0.0), axis=-1, keepdims=True)
    s_hi = jnp.sum(jnp.where(lo, 0.0, sq), axis=-1, keepdims=True)
    ms = jnp.where(lo, s_lo, s_hi) * (1.0 / MLA_ROPE)
    y = xp * lax.rsqrt(ms + EPS) * g_tile
    up = pltpu.roll(y, LANES - ROPE_HALF, axis=1)
    dn = pltpu.roll(y, ROPE_HALF, axis=1)
    partner = jnp.where((lane % MLA_ROPE) < ROPE_HALF, up, dn)
    return y * cos + partner * sin


def _prep_kernel(p_ref, cos_ref, sin_ref, qn_ref, wuq_ref, kvn_ref, wukv_ref, gqn_ref, gqp_ref,
                 gkn_ref, gkp_ref, wa2_ref, ba_ref, q_ref, k_ref, v_ref, la_ref):
    tm = p_ref.shape[0]
    lane = lax.broadcasted_iota(I32, (tm, LANES), 1)
    cos = cos_ref[...]
    sin = sin_ref[...]
    scale = (MLA_NOPE + MLA_ROPE) ** -0.5

    cq = p_ref[:, COL_CQ:COL_CQ + MLA_Q_RANK].astype(F32)
    qa = _dot(_rms(cq, qn_ref[...]).astype(BF16), wuq_ref[...])
    ckv = p_ref[:, COL_CKV:COL_CKV + MLA_KV_RANK].astype(F32)
    kva = _dot(_rms(ckv, kvn_ref[...]).astype(BF16), wukv_ref[...])

    misc = p_ref[:, COL_KPE:COL_KPE + LANES]
    kpe = _rope_pair_tile(misc.astype(F32), gkp_ref[...], cos, sin, lane)
    kpe = jnp.where(lane < MLA_ROPE, kpe, 0.0).astype(BF16)

    pe_off = MLA_HEADS * MLA_NOPE
    for j in range(MLA_HEADS // 2):
        t = _rope_pair_tile(qa[:, pe_off + j * LANES: pe_off + (j + 1) * LANES], gqp_ref[...], cos, sin, lane)
        t = t * scale
        even = jnp.where(lane < MLA_ROPE, t, 0.0)
        odd = jnp.where(lane < MLA_ROPE, pltpu.roll(t, MLA_ROPE, axis=1), 0.0)
        for h, pe in ((2 * j, even), (2 * j + 1, odd)):
            qn = _rms(qa[:, h * MLA_NOPE:(h + 1) * MLA_NOPE], gqn_ref[...]) * scale
            q_ref[:, h * MLA_QK_PAD: h * MLA_QK_PAD + MLA_NOPE] = qn.astype(BF16)
            q_ref[:, h * MLA_QK_PAD + MLA_NOPE:(h + 1) * MLA_QK_PAD] = pe.astype(BF16)
    for h in range(MLA_HEADS):
        kn = _rms(kva[:, h * MLA_NOPE:(h + 1) * MLA_NOPE], gkn_ref[...])
        k_ref[:, h * MLA_QK_PAD: h * MLA_QK_PAD + MLA_NOPE] = kn.astype(BF16)
        k_ref[:, h * MLA_QK_PAD + MLA_NOPE:(h + 1) * MLA_QK_PAD] = kpe
    v_off = MLA_HEADS * MLA_NOPE
    for h in range(MLA_HEADS):
        v_ref[:, h * MLA_V_EXT: h * MLA_V_EXT + MLA_V] = kva[:, v_off + h * MLA_V: v_off + (h + 1) * MLA_V].astype(BF16)
        v_ref[:, h * MLA_V_EXT + MLA_V:(h + 1) * MLA_V_EXT] = jnp.ones((tm, MLA_V_EXT - MLA_V), BF16)

    z = _dot(misc, wa2_ref[...]) + ba_ref[...]
    log_sig = jnp.minimum(z, 0.0) - jnp.log(1.0 + jnp.exp(-jnp.abs(z)))
    la_ref[...] = log_sig * (1.0 / GLA_TAU)


def _prep(proj, cos_t, sin_t, wts, tm, rows_per_seq):
    m = proj.shape[0]
    nblk = rows_per_seq // tm
    full = lambda a: pl.BlockSpec(a.shape, lambda i: (0,) * a.ndim)
    return pl.pallas_call(
        _prep_kernel,
        grid=(m // tm,),
        in_specs=[
            pl.BlockSpec((tm, 1024), lambda i: (i, 0)),
            pl.BlockSpec((tm, LANES), lambda i: (i % nblk, 0)),
            pl.BlockSpec((tm, LANES), lambda i: (i % nblk, 0)),
        ] + [full(w) for w in wts],
        out_specs=[
            pl.BlockSpec((tm, MLA_HEADS * MLA_QK_PAD), lambda i: (i, 0)),
            pl.BlockSpec((tm, MLA_HEADS * MLA_QK_PAD), lambda i: (i, 0)),
            pl.BlockSpec((tm, MLA_HEADS * MLA_V_EXT), lambda i: (i, 0)),
            pl.BlockSpec((tm, GLA_DK), lambda i: (i, 0)),
        ],
        out_shape=[
            jax.ShapeDtypeStruct((m, MLA_HEADS * MLA_QK_PAD), BF16),
            jax.ShapeDtypeStruct((m, MLA_HEADS * MLA_QK_PAD), BF16),
            jax.ShapeDtypeStruct((m, MLA_HEADS * MLA_V_EXT), BF16),
            jax.ShapeDtypeStruct((m, GLA_DK), F32),
        ],
        compiler_params=pltpu.CompilerParams(vmem_limit_bytes=VMEM_LIMIT),
        name="mla_gla_prep",
    )(proj, cos_t, sin_t, *wts)


def _attn_kernel(q_ref, k_ref, v_ref, km_ref, vm_ref, o_ref, *, tq, tk):
    i = pl.program_id(1)
    qs = [slice(h * MLA_QK_PAD, (h + 1) * MLA_QK_PAD) for h in range(MLA_HEADS)]
    vs = [slice(h * MLA_V_EXT, (h + 1) * MLA_V_EXT) for h in range(MLA_HEADS)]

    def update(h, state, kblk, vblk, mask):
        m, acc = state
        s = _dot_nt(q_ref[0, :, qs[h]], kblk)
        if mask is not None:
            s = jnp.where(mask, s, -jnp.inf)
        m_new = jnp.maximum(m, jnp.max(s, axis=-1, keepdims=True))
        p = jnp.exp(s - m_new).astype(BF16)
        return m_new, jnp.exp(m - m_new) * acc + _dot(p, vblk)

    meta_mask = lax.broadcasted_iota(I32, (tq, LANES), 1) < N_META
    init = (jnp.full((tq, 1), -jnp.inf, F32), jnp.zeros((tq, MLA_V_EXT), F32))
    states = tuple(update(h, init, km_ref[:, qs[h]], vm_ref[:, vs[h]], meta_mask) for h in range(MLA_HEADS))

    def body(j, states):
        r0 = pl.multiple_of(j * tk, tk)
        return tuple(update(h, states[h], k_ref[0, pl.ds(r0, tk), qs[h]], v_ref[0, pl.ds(r0, tk), vs[h]], None)
                     for h in range(MLA_HEADS))

    nfull = (i * tq) // tk
    states = lax.fori_loop(0, nfull, body, states)
    r0 = pl.multiple_of(nfull * tk, tk)
    q_chunk = (i * tq + lax.broadcasted_iota(I32, (tq, tk), 0)) // CHUNK
    k_chunk = (r0 + lax.broadcasted_iota(I32, (tq, tk), 1)) // CHUNK
    last_mask = k_chunk <= q_chunk
    for h in range(MLA_HEADS):
        _, acc = update(h, states[h], k_ref[0, pl.ds(r0, tk), qs[h]], v_ref[0, pl.ds(r0, tk), vs[h]], last_mask)
        o_ref[0, :, h * MLA_V:(h + 1) * MLA_V] = (acc[:, :MLA_V] / acc[:, MLA_V:]).astype(o_ref.dtype)


def _attention(q, k, v, k_meta, v_meta, tq, tk):
    b, r, _ = q.shape
    assert tk % tq == 0 and r % tk == 0
    return pl.pallas_call(
        functools.partial(_attn_kernel, tq=tq, tk=tk),
        grid=(b, r // tq),
        in_specs=[
            pl.BlockSpec((1, tq, q.shape[2]), lambda bi, i: (bi, i, 0)),
            pl.BlockSpec((1, r, k.shape[2]), lambda bi, i: (bi, 0, 0)),
            pl.BlockSpec((1, r, v.shape[2]), lambda bi, i: (bi, 0, 0)),
            pl.BlockSpec(k_meta.shape, lambda bi, i: (0, 0)),
            pl.BlockSpec(v_meta.shape, lambda bi, i: (0, 0)),
        ],
        out_specs=pl.BlockSpec((1, tq, MLA_HEADS * MLA_V), lambda bi, i: (bi, i, 0)),
        out_shape=jax.ShapeDtypeStruct((b, r, MLA_HEADS * MLA_V), BF16),
        compiler_params=pltpu.CompilerParams(vmem_limit_bytes=VMEM_LIMIT),
        name="mla_attention",
    )(q, k, v, k_meta, v_meta)


def _gla_level_consts(c):
    idx = np.arange(c)
    tril = (idx[None, :] <= idx[:, None]).astype(np.float32)
    sels, masks = [], []
    m = c // 2
    while m >= 1:
        ref = (idx // (2 * m)) * (2 * m) + m - 1
        sels.append(tril[ref])
        same = (idx[:, None] // (2 * m)) == (idx[None, :] // (2 * m))
        masks.append(same & ((idx[:, None] % (2 * m)) >= m) & ((idx[None, :] % (2 * m)) < m))
        m //= 2
    return tril, np.stack(sels), np.stack(masks).astype(np.float32)


def _gla_kernel(gq_ref, gk_ref, gv_ref, gg_ref, la_ref, s0_ref, gn_ref, tril_ref, sel_ref, msk_ref,
                y_ref, sout_ref, s_scr, *, c, nlev):
    ci = pl.program_id(1)

    @pl.when(ci == 0)
    def _():
        s_scr[...] = s0_ref[...]

    la = la_ref[...]
    hi = la.astype(BF16)
    rest = la - hi.astype(F32)
    mid = rest.astype(BF16)
    lo = (rest - mid.astype(F32)).astype(BF16)
    csel = jnp.concatenate([tril_ref[...]] + [sel_ref[lv] for lv in range(nlev)], axis=0).astype(BF16)
    cum = _dot(csel, hi) + _dot(csel, mid) + _dot(csel, lo)
    b_all = cum[:c]
    brefs = [cum[(lv + 1) * c:(lv + 2) * c] for lv in range(nlev)]
    eye = (lax.broadcasted_iota(I32, (c, c), 0) == lax.broadcasted_iota(I32, (c, c), 1))
    eye_k = (lax.broadcasted_iota(I32, (GLA_HK, GLA_HK), 0) == lax.broadcasted_iota(I32, (GLA_HK, GLA_HK), 1))

    for h in range(GLA_HEADS):
        ks = slice(h * GLA_HK, (h + 1) * GLA_HK)
        vs = slice(h * GLA_HV, (h + 1) * GLA_HV)
        q = gq_ref[:, ks].astype(F32) * (GLA_HK ** -0.5)
        k = gk_ref[:, ks].astype(F32)
        v = gv_ref[:, vs]
        bh = b_all[:, ks]
        s_old = s_scr[h]

        o = _dot((q * jnp.exp(bh)).astype(BF16), s_old.astype(BF16))
        a = jnp.where(eye, _dot_nt(q.astype(BF16), k.astype(BF16)), 0.0)
        for lv in range(nlev):
            br = brefs[lv][:, ks]
            qd = q * jnp.exp(jnp.minimum(bh - br, 0.0))
            kd = k * jnp.exp(jnp.minimum(br - bh, 0.0))
            a = a + msk_ref[lv] * _dot_nt(qd.astype(BF16), kd.astype(BF16))
        o = o + _dot(a.astype(BF16), v)

        b_last = bh[c - 1:c, :]
        kdec = k * jnp.exp(b_last - bh)
        dec_col = jnp.sum(jnp.where(eye_k, jnp.exp(b_last), 0.0), axis=1, keepdims=True)
        s_scr[h] = dec_col * s_old + _dot_tn(kdec.astype(BF16), v)

        gg = gg_ref[:, vs].astype(F32)
        silu = gg / (1.0 + jnp.exp(-gg))
        y_ref[:, vs] = (_rms(o, gn_ref[...]) * silu).astype(y_ref.dtype)

    @pl.when(ci == pl.num_programs(1) - 1)
    def _():
        sout_ref[0] = s_scr[...]


def _gla(proj, log_a, s0, g_norm, nb, c):
    m = proj.shape[0]
    nc = m // nb // c
    tril, sels, masks = _gla_level_consts(c)
    nlev = sels.shape[0]
    row = lambda bi, ci: bi * nc + ci
    return pl.pallas_call(
        functools.partial(_gla_kernel, c=c, nlev=nlev),
        grid=(nb, nc),
        in_specs=[
            pl.BlockSpec((c, GLA_DK), lambda bi, ci: (row(bi, ci), COL_GQ // GLA_DK)),
            pl.BlockSpec((c, GLA_DK), lambda bi, ci: (row(bi, ci), COL_GK // GLA_DK)),
            pl.BlockSpec((c, GLA_DV), lambda bi, ci: (row(bi, ci), COL_GV // GLA_DV)),
            pl.BlockSpec((c, GLA_DV), lambda bi, ci: (row(bi, ci), COL_GG // GLA_DV)),
            pl.BlockSpec((c, GLA_DK), lambda bi, ci: (row(bi, ci), 0)),
            pl.BlockSpec(s0.shape, lambda bi, ci: (0, 0, 0)),
            pl.BlockSpec(g_norm.shape, lambda bi, ci: (0, 0)),
            pl.BlockSpec(tril.shape, lambda bi, ci: (0, 0)),
            pl.BlockSpec(sels.shape, lambda bi, ci: (0, 0, 0)),
            pl.BlockSpec(masks.shape, lambda bi, ci: (0, 0, 0)),
        ],
        out_specs=[
            pl.BlockSpec((c, GLA_DV), lambda bi, ci: (row(bi, ci), 0)),
            pl.BlockSpec((1,) + s0.shape, lambda bi, ci: (bi, 0, 0, 0)),
        ],
        out_shape=[
            jax.ShapeDtypeStruct((m, GLA_DV), BF16),
            jax.ShapeDtypeStruct((nb,) + s0.shape, F32),
        ],
        scratch_shapes=[pltpu.VMEM(s0.shape, F32)],
        compiler_params=pltpu.CompilerParams(vmem_limit_bytes=VMEM_LIMIT),
        name="gla",
    )(proj, proj, proj, proj, log_a, s0, g_norm, jnp.asarray(tril), jnp.asarray(sels), jnp.asarray(masks))


def _mix_kernel(ya_ref, yb_ref, ga_ref, gb_ref, x_ref, wa_ref, wb_ref, wo_ref, gf_ref, wq_ref, keys_ref,
                h1_ref, hn_ref, st_ref):
    sig = lambda t: 1.0 / (1.0 + jnp.exp(-t))
    a = _dot(ya_ref[...], wa_ref[...])
    b = _dot(yb_ref[...], wb_ref[...])
    mix = sig(ga_ref[...].astype(F32)) * a + sig(gb_ref[...].astype(F32)) * b
    h1 = x_ref[...] + _dot(mix.astype(BF16), wo_ref[...])
    h1_ref[...] = h1
    hn = _rms(h1, gf_ref[...]).astype(BF16)
    hn_ref[...] = hn
    qp = _dot(hn, wq_ref[...]).astype(BF16)
    for hp in range(2 * PEER_HEADS):
        st_ref[hp] = _dot_nt(keys_ref[hp % 2], qp[:, hp * PEER_HALF:(hp + 1) * PEER_HALF])


def _mix(y_a, y_b, proj, x2d, w_a, w_b, w_o, g_ffn, w_q, keys, tm):
    m = x2d.shape[0]
    full = lambda a: pl.BlockSpec(a.shape, lambda i: (0,) * a.ndim)
    return pl.pallas_call(
        _mix_kernel,
        grid=(m // tm,),
        in_specs=[
            pl.BlockSpec((tm, D_MODEL), lambda i: (i, 0)),
            pl.BlockSpec((tm, D_MODEL), lambda i: (i, 0)),
            pl.BlockSpec((tm, D_MODEL), lambda i: (i, COL_GATE_A // D_MODEL)),
            pl.BlockSpec((tm, D_MODEL), lambda i: (i, COL_GATE_B // D_MODEL)),
            pl.BlockSpec((tm, D_MODEL), lambda i: (i, 0)),
            full(w_a), full(w_b), full(w_o), full(g_ffn), full(w_q), full(keys),
        ],
        out_specs=[
            pl.BlockSpec((tm, D_MODEL), lambda i: (i, 0)),
            pl.BlockSpec((tm, D_MODEL), lambda i: (i, 0)),
            pl.BlockSpec((2 * PEER_HEADS, PEER_NKEYS, tm), lambda i: (0, 0, i)),
        ],
        out_shape=[
            jax.ShapeDtypeStruct((m, D_MODEL), F32),
            jax.ShapeDtypeStruct((m, D_MODEL), BF16),
            jax.ShapeDtypeStruct((2 * PEER_HEADS, PEER_NKEYS, m), F32),
        ],
        compiler_params=pltpu.CompilerParams(vmem_limit_bytes=VMEM_LIMIT),
        name="mix_out_peer_scores",
    )(y_a, y_b, proj, proj, x2d, w_a, w_b, w_o, g_ffn, w_q, keys)


def _topk_rows(s, k, n):
    half = n // 2
    a, b = s[:half], s[half:]
    ra = lax.broadcasted_iota(I32, a.shape, 0).astype(F32)
    rb = ra + float(half)
    a_wins = a >= b
    w, l = jnp.where(a_wins, a, b), jnp.where(a_wins, b, a)
    wi, li = jnp.where(a_wins, ra, rb), jnp.where(a_wins, rb, ra)
    vals, ids = [], []
    for _ in range(k):
        m = jnp.max(w, axis=0, keepdims=True)
        am = jnp.min(jnp.where(w == m, wi, float(n)), axis=0, keepdims=True)
        vals.append(m)
        ids.append(am)
        hit = wi == am
        w, wi, l = jnp.where(hit, l, w), jnp.where(hit, li, wi), jnp.where(hit, -jnp.inf, l)
    return jnp.concatenate(vals, axis=0), jnp.concatenate(ids, axis=0)


def _topk_head(st_ref, h, idx_ref, gate_ref):
    state = None
    for part in range(TOPK_PARTS):
        state = _topk_head_part(st_ref, h, idx_ref, gate_ref, part, state)


TOPK_PARTS = 4


def _topk_head_part(st_ref, h, idx_ref, gate_ref, part, state):
    if part == 0:
        return _topk_rows(st_ref[2 * h], PEER_TOPK, PEER_NKEYS)
    if part == 1:
        return state + _topk_rows(st_ref[2 * h + 1], PEER_TOPK, PEER_NKEYS)
    if part == 2:
        cs, ci, pos = _topk_candidates(*state)
        return _topk_extract(cs, ci, pos, [], [], PEER_TOPK // 2)
    cs, ci, pos, vals, ids = _topk_extract(*state, PEER_TOPK // 2)
    best = jnp.concatenate(vals, axis=0)
    e = jnp.exp(best - best[0:1])
    gate_ref[h] = e / jnp.sum(e, axis=0, keepdims=True)
    idx_ref[h] = jnp.concatenate(ids, axis=0).astype(I32)
    return None


def _topk_extract(cs, ci, pos, vals, ids, steps):
    n = float(PEER_TOPK * PEER_TOPK)
    vals, ids = list(vals), list(ids)
    for _ in range(steps):
        m = jnp.max(cs, axis=0, keepdims=True)
        ap = jnp.min(jnp.where(cs == m, pos, n), axis=0, keepdims=True)
        sel = pos == ap
        vals.append(m)
        ids.append(jnp.sum(jnp.where(sel, ci, 0.0), axis=0, keepdims=True))
        cs = jnp.where(sel, -jnp.inf, cs)
    return cs, ci, pos, vals, ids


def _topk_candidates(v1, i1, v2, i2):
    half = PEER_TOPK // 2
    sel = [(slice(0, 1), slice(None))] + [(slice(a, a + 1), slice(0, half)) for a in range(1, half)]
    sel.append((slice(half, None), slice(0, 1)))
    cs = jnp.concatenate([v1[sa] + v2[sb] for sa, sb in sel], axis=0)
    ci = jnp.concatenate([i1[sa] * PEER_NKEYS + i2[sb] for sa, sb in sel], axis=0)
    r = lax.broadcasted_iota(I32, cs.shape, 0)
    mid = PEER_TOPK + (r - PEER_TOPK) // half * PEER_TOPK + (r - PEER_TOPK) % half
    tail = (r - (PEER_TOPK + (half - 1) * half) + half) * PEER_TOPK
    pos = jnp.where(r < PEER_TOPK, r, jnp.where(r < PEER_TOPK + (half - 1) * half, mid, tail))
    return cs, ci, pos.astype(F32)


def _pack_kernel(x_ref, o_ref):
    rows = x_ref.shape[0]
    x = x_ref[...].astype(F32).reshape(rows * ROW_BF16, LANES)
    o_ref[...] = pltpu.bitcast(x.astype(BF16), I32)


def _pack_rows(a, rows):
    n = a.shape[0]
    return pl.pallas_call(
        _pack_kernel,
        grid=(n // rows,),
        in_specs=[pl.BlockSpec((rows, a.shape[1]), lambda i: (i, 0))],
        out_specs=pl.BlockSpec((rows * ROW_WORDS, LANES), lambda i: (i, 0)),
        out_shape=jax.ShapeDtypeStruct((n * ROW_WORDS, LANES), I32),
        compiler_params=pltpu.CompilerParams(vmem_limit_bytes=VMEM_LIMIT),
        name="pack_rows",
    )(a)


def _lane_group_consts():
    lane = np.arange(PEER_PAIRS * ROW_BF16)
    diag = (lane[None, :] % ROW_BF16) == np.arange(ROW_BF16)[:, None]
    group = (lane[:, None] // ROW_BF16) == np.arange(PEER_PAIRS)[None, :]
    return jnp.asarray(diag, F32), jnp.asarray(group, BF16), jnp.asarray(group.T, BF16)


GROUP = 8
TOKEN_WORDS = PEER_PAIRS * ROW_WORDS
GROUP_IDX = GROUP * PEER_PAIRS
NWIN = 4


def _gather_group(tab_v, win, stage):
    for j in range(GROUP_IDX):
        e = pl.multiple_of(win[j // PEER_PAIRS, j % PEER_PAIRS], ROW_WORDS)
        stage[j * ROW_WORDS:(j + 1) * ROW_WORDS, :] = tab_v[pl.ds(e, ROW_WORDS), :]


def _token_rows(stage, j):
    return pltpu.bitcast(stage[j * TOKEN_WORDS:(j + 1) * TOKEN_WORDS, :], BF16)


def _dot_hi_lo(x, w):
    n = x.shape[0]
    hi = x.astype(BF16).astype(F32)
    r = _dot(jnp.concatenate([hi, x - hi], axis=0).astype(BF16), w)
    return r[:n] + r[n:]


def _pipelined_groups(tab_hbm, tab_v, wins, stages, tab_sem, win_sems, tt, idx_rows, compute, side=None):
    i = pl.program_id(0)
    nsteps = pl.num_programs(0)
    ng = tt // GROUP
    assert ng % NWIN == 0 and ng >= 2 * NWIN and NWIN % len(stages) == 0

    def dma(w, g):
        return pltpu.make_async_copy(idx_rows(g), wins[w % NWIN], win_sems.at[w % NWIN])

    @pl.when(i == 0)
    def _():
        table = pltpu.make_async_copy(tab_hbm, tab_v, tab_sem)
        table.start()
        for w in range(NWIN):
            dma(w, w).start()
        table.wait()

    def gather_next_and_compute(g, u):
        dma(u + 1, g + 1).wait()
        dma(u, g + NWIN).start()
        _gather_group(tab_v, wins[(u + 1) % NWIN], stages[(u + 1) % len(stages)])
        compute(stages[u % len(stages)], g)

    dma(0, 0).wait()
    _gather_group(tab_v, wins[0], stages[0])
    if side:
        side[0]()

    def quad(k, carry):
        state = None
        for u in range(NWIN):
            gather_next_and_compute(NWIN * k + u, u)
            if side:
                state = side[1](k, u, state)
        return carry

    lax.fori_loop(0, ng // NWIN - 1, quad, 0)
    if side:
        side[2]()
    for u in range(NWIN - 1):
        gather_next_and_compute(ng - NWIN + u, u)
    compute(stages[(NWIN - 1) % len(stages)], ng - 1)
    dma(NWIN - 1, ng - 1 + NWIN).start()

    @pl.when(i == nsteps - 1)
    def _():
        for w in range(NWIN):
            dma(w, 0).wait()


def _peer_u_kernel(tab_hbm, hp_ref, st0_ref, stn_ref, diag_ref, group_ref, coef_ref, idx_out_ref,
                   tab_v, win0, win1, win2, win3, stage_a, stage_b, act, idx_v, gate_v, tk_idx, tk_gate,
                   tab_sem, win_sems, *, tt):
    i = pl.program_id(0)
    ng = tt // GROUP
    cur, nxt = i % 2, (i + 1) % 2
    assert ng // NWIN == PEER_HEADS

    def finish_topk(slot):
        idx_v[slot] = tk_idx[...].reshape(PEER_PAIRS, tt).T * ROW_WORDS
        gate_v[slot] = tk_gate[...].reshape(PEER_PAIRS, tt).T

    @pl.when(i == 0)
    def _():
        def head(h, carry):
            _topk_head(st0_ref, h, tk_idx, tk_gate)
            return carry
        lax.fori_loop(0, PEER_HEADS, head, 0)
        finish_topk(0)

    def idx_rows(g):
        if isinstance(g, int):
            slot, g = (nxt, g - ng) if g >= ng else (cur, g)
            return idx_v.at[slot, pl.ds(g * GROUP, GROUP), :]
        return idx_v.at[cur, pl.ds(pl.multiple_of(g * GROUP, GROUP), GROUP), :]

    def compute(stage, g):
        zs = []
        for j in range(GROUP):
            r0 = pl.multiple_of((g * GROUP + j) * ROW_WORDS, ROW_WORDS)
            ht = pltpu.bitcast(hp_ref[pl.ds(r0, ROW_WORDS), :], BF16)
            zs.append(_dot_nt(ht, _token_rows(stage, j)) * diag_ref[...])
        a = _dot_hi_lo(jnp.concatenate(zs, axis=0), group_ref[...])
        acts = [jnp.sum(a[j * ROW_BF16:(j + 1) * ROW_BF16], axis=0, keepdims=True) for j in range(GROUP)]
        rows = pl.ds(pl.multiple_of(g * GROUP, GROUP), GROUP)
        act[rows, :] = jnp.concatenate(acts, axis=0)

    side = (
        lambda: _topk_head(stn_ref, PEER_HEADS - 1, tk_idx, tk_gate),
        lambda k, u, state: _topk_head_part(stn_ref, k, tk_idx, tk_gate, u, state),
        lambda: finish_topk(nxt),
    )
    _pipelined_groups(tab_hbm, tab_v, (win0, win1, win2, win3), (stage_a, stage_b), tab_sem, win_sems, tt,
                      idx_rows, compute, side)
    idx_out_ref[...] = idx_v[cur]
    a = act[...]
    gelu = 0.5 * a * (1.0 + lax.erf(a * (2.0 ** -0.5)))
    coef_ref[...] = gelu * gate_v[cur]


def _peer_v_kernel(idx_hbm, tab_hbm, coef_ref, h1_ref, diag_ref, group_t_ref, y_ref,
                   tab_v, win0, win1, win2, win3, stage_a, stage_b, tab_sem, win_sems, *, tt):
    i = pl.program_id(0)
    last_group = pl.num_programs(0) * (tt // GROUP) - 1

    def idx_rows(g):
        group = jnp.minimum(i * (tt // GROUP) + g, last_group)
        return idx_hbm.at[pl.ds(pl.multiple_of(group * GROUP, GROUP), GROUP), :]

    def compute(stage, g):
        rows8 = pl.ds(pl.multiple_of(g * GROUP, GROUP), GROUP)
        crep = _dot_hi_lo(coef_ref[rows8, :], group_t_ref[...])
        outs = []
        for j in range(GROUP):
            c = jnp.broadcast_to(crep[j:j + 1, :], (ROW_BF16, crep.shape[1])) * diag_ref[...]
            outs.append(_dot_hi_lo(c, _token_rows(stage, j)))
        y_ref[rows8, :] = h1_ref[rows8, :] + jnp.stack(outs).reshape(GROUP, D_MODEL)

    _pipelined_groups(tab_hbm, tab_v, (win0, win1, win2, win3), (stage_a, stage_b), tab_sem, win_sems, tt,
                      idx_rows, compute)


def _peer_scratch(tt, u_phase):
    shapes = [pltpu.VMEM((PEER_EXPERTS * ROW_WORDS, LANES), I32)]
    shapes += [pltpu.SMEM((GROUP, PEER_PAIRS), I32)] * NWIN
    shapes += [pltpu.VMEM((GROUP * TOKEN_WORDS, LANES), I32)] * 2
    if u_phase:
        shapes += [
            pltpu.VMEM((tt, PEER_PAIRS), F32),
            pltpu.VMEM((2, tt, PEER_PAIRS), I32),
            pltpu.VMEM((2, tt, PEER_PAIRS), F32),
            pltpu.VMEM((PEER_HEADS, PEER_TOPK, tt), I32),
            pltpu.VMEM((PEER_HEADS, PEER_TOPK, tt), F32),
        ]
    return shapes + [pltpu.SemaphoreType.DMA, pltpu.SemaphoreType.DMA((NWIN,))]


def _peer_u(tab, hn_packed, st, tt):
    t = st.shape[2]
    n = t // tt
    diag, group, _ = _lane_group_consts()
    st_block = (2 * PEER_HEADS, PEER_NKEYS, tt)
    return pl.pallas_call(
        functools.partial(_peer_u_kernel, tt=tt),
        grid=(n,),
        in_specs=[
            pl.BlockSpec(memory_space=pl.ANY),
            pl.BlockSpec((tt * ROW_WORDS, LANES), lambda i: (i, 0)),
            pl.BlockSpec(st_block, lambda i: (0, 0, 0)),
            pl.BlockSpec(st_block, lambda i: (0, 0, jnp.minimum(i + 1, n - 1))),
            pl.BlockSpec(diag.shape, lambda i: (0, 0)),
            pl.BlockSpec(group.shape, lambda i: (0, 0)),
        ],
        out_specs=[
            pl.BlockSpec((tt, PEER_PAIRS), lambda i: (i, 0)),
            pl.BlockSpec((tt, PEER_PAIRS), lambda i: (i, 0)),
        ],
        out_shape=[
            jax.ShapeDtypeStruct((t, PEER_PAIRS), F32),
            jax.ShapeDtypeStruct((t, PEER_PAIRS), I32),
        ],
        scratch_shapes=_peer_scratch(tt, True),
        compiler_params=pltpu.CompilerParams(vmem_limit_bytes=VMEM_LIMIT),
        name="peer_u",
    )(tab, hn_packed, st, st, diag, group)


def _peer_v(idx_flat, tab, coef, h1_rows, tt):
    t = coef.shape[0]
    diag, _, group_t = _lane_group_consts()
    any_spec = pl.BlockSpec(memory_space=pl.ANY)
    return pl.pallas_call(
        functools.partial(_peer_v_kernel, tt=tt),
        grid=(t // tt,),
        in_specs=[
            any_spec, any_spec,
            pl.BlockSpec((tt, PEER_PAIRS), lambda i: (i, 0)),
            pl.BlockSpec((tt, D_MODEL), lambda i: (i, 0)),
            pl.BlockSpec(diag.shape, lambda i: (0, 0)),
            pl.BlockSpec(group_t.shape, lambda i: (0, 0)),
        ],
        out_specs=pl.BlockSpec((tt, D_MODEL), lambda i: (i, 0)),
        out_shape=jax.ShapeDtypeStruct((t, D_MODEL), F32),
        scratch_shapes=_peer_scratch(tt, False),
        compiler_params=pltpu.CompilerParams(vmem_limit_bytes=VMEM_LIMIT),
        name="peer_v",
    )(idx_flat, tab, coef, h1_rows, diag, group_t)


def _rope_tables(pos):
    inv_freq = ROPE_THETA ** (-jnp.arange(ROPE_HALF, dtype=F32) / ROPE_HALF)
    ang = pos.astype(F32)[:, None] * inv_freq[None, :]
    cos, sin = jnp.cos(ang), jnp.sin(ang)
    cos_t = jnp.tile(cos, (1, LANES // ROPE_HALF))
    sin_t = jnp.tile(jnp.concatenate([-sin, sin], axis=1), (1, LANES // MLA_ROPE))
    return cos_t, sin_t


def _pack_in_proj(w_in):
    offs = np.cumsum([0, MLA_Q_RANK, MLA_KV_RANK, MLA_ROPE, GLA_DK, GLA_DK, GLA_DV, GLA_GATE_RANK, GLA_DV,
                      D_MODEL, D_MODEL])
    part = lambda i: w_in[:, int(offs[i]):int(offs[i + 1])]
    zeros = lambda n: jnp.zeros((D_MODEL, n), w_in.dtype)
    cols = [part(0), part(1), part(2), part(6), zeros(LANES - MLA_ROPE - GLA_GATE_RANK), zeros(COL_GQ - COL_KPE - LANES),
            part(3), part(4), part(5), part(7), part(8), part(9)]
    w = jnp.concatenate(cols, axis=1).astype(BF16)
    assert w.shape[1] == PROJ_COLS
    return w


def _head_major(w, widths):
    per = sum(widths)
    w = w.reshape(w.shape[0], -1, per)
    parts, o = [], 0
    for wd in widths:
        parts.append(w[:, :, o:o + wd].reshape(w.shape[0], -1))
        o += wd
    return jnp.concatenate(parts, axis=1)


class _Tiles(NamedTuple):
    inproj_rows: int
    inproj_cols: int
    prep_rows: int
    attn_q: int
    attn_k: int
    gla_chunk: int
    mix_rows: int
    pack_rows: int
    peer_u_tokens: int
    peer_v_tokens: int


def _tiles(t):
    rows = min(t, 512)
    return _Tiles(rows, PROJ_COLS // 2, 256, 256, 512, 2 * CHUNK, rows, 512, NWIN * GROUP * PEER_HEADS, 256)


def kernel(x, meta, norm_mix, w_in, mla_q_norm, mla_w_uq, mla_kv_norm, mla_w_ukv, qn_nope, qn_pe, kn_nope, kn_pe, gla_w_a2, gla_b_a, gla_norm, w_o_mla, w_o_gla, w_out, norm_ffn, peer_w_q, peer_keys, peer_u, peer_v):
    nb, r, d = x.shape
    assert d == D_MODEL and norm_mix.shape[0] == 1 and meta.shape == (N_META, D_MODEL)
    t = nb * r
    tl = _tiles(t)
    assert r % tl.attn_k == 0 and r % tl.gla_chunk == 0 and r % tl.prep_rows == 0
    assert all(t % n == 0 for n in (tl.inproj_rows, tl.mix_rows, tl.pack_rows, tl.peer_u_tokens, tl.peer_v_tokens))
    x2d = x.reshape(t, d)
    row = lambda a: a.reshape(1, -1).astype(F32)

    w1 = _pack_in_proj(w_in[0])
    g_mix = row(norm_mix[0])
    tile2 = lambda g: jnp.tile(row(g), (1, LANES // MLA_ROPE))
    gkp = jnp.concatenate([row(kn_pe[0]), jnp.ones((1, LANES - MLA_ROPE), F32)], axis=1)
    wa2 = jnp.zeros((LANES, GLA_DK), F32).at[MLA_ROPE:MLA_ROPE + GLA_GATE_RANK].set(gla_w_a2[0]).astype(BF16)
    prep_w = (
        row(mla_q_norm[0]), _head_major(mla_w_uq[0], (MLA_NOPE, MLA_ROPE)).astype(BF16),
        row(mla_kv_norm[0]), _head_major(mla_w_ukv[0], (MLA_NOPE, MLA_V)).astype(BF16),
        row(qn_nope[0]), tile2(qn_pe[0]), row(kn_nope[0]), gkp, wa2, row(gla_b_a[0]),
    )
    g_gla = row(gla_norm[0])

    proj_m = _inproj(meta.astype(F32), g_mix, w1, N_META, tl.inproj_cols)
    cos_m, sin_m = _rope_tables(jnp.arange(N_META))
    _, k_m, v_m, la_m = _prep(proj_m, cos_m, sin_m, prep_w, N_META, N_META)
    s_zero = jnp.zeros((GLA_HEADS, GLA_HK, GLA_HV), F32)
    _, s_meta = _gla(proj_m, la_m, s_zero, g_gla, 1, N_META)
    pad = ((0, LANES - N_META), (0, 0))
    k_m, v_m = jnp.pad(k_m, pad), jnp.pad(v_m, pad)

    proj = _inproj(x2d, g_mix, w1, tl.inproj_rows, tl.inproj_cols)
    cos_t, sin_t = _rope_tables(N_META + jnp.arange(r))
    q, k, v, log_a = _prep(proj, cos_t, sin_t, prep_w, tl.prep_rows, r)
    y_a = _attention(q.reshape(nb, r, -1), k.reshape(nb, r, -1), v.reshape(nb, r, -1), k_m, v_m,
                     tl.attn_q, tl.attn_k)
    y_b, _ = _gla(proj, log_a, s_meta[0], g_gla, nb, tl.gla_chunk)

    h1, hn, st = _mix(y_a.reshape(t, -1), y_b, proj, x2d, w_o_mla[0].astype(BF16), w_o_gla[0].astype(BF16),
                      w_out[0].astype(BF16), row(norm_ffn[0]), peer_w_q[0].astype(BF16),
                      peer_keys[0].astype(BF16), tl.mix_rows)
    coef, idx = _peer_u(_pack_rows(peer_u[0], tl.pack_rows), _pack_rows(hn, tl.pack_rows), st, tl.peer_u_tokens)
    y = _peer_v(idx, _pack_rows(peer_v[0], tl.pack_rows), coef, h1, tl.peer_v_tokens)
    return y.reshape(nb, r, d)
```
